```python
import math
import jax, jax.numpy as jnp
from jax import lax
import numpy as np

D_MODEL = 1024
BATCH = 16
SEQ = 2048
DEPTH = 1

D_MIX = D_MODEL
D_SSM = D_MIX // 2
SSM_GROUP = 16
N_SSM_GROUPS = D_SSM // SSM_GROUP
SSM_STATE = 64
D_ATTN = D_MIX - D_SSM
N_HEADS = 8
QK_NOPE_DIM = 64
QK_ROPE_DIM = 32
QK_HEAD_DIM = QK_NOPE_DIM + QK_ROPE_DIM
V_HEAD_DIM = D_ATTN // N_HEADS
Q_LORA_RANK = 256
KV_LORA_RANK = 128
ROPE_THETA = 10000.0
Q_BLOCK = 128
D_IN_PROJ = D_SSM + Q_LORA_RANK + KV_LORA_RANK + QK_ROPE_DIM
PEER_HEADS = 8
PEER_N_KEYS = 128
PEER_N_EXPERTS = PEER_N_KEYS * PEER_N_KEYS
PEER_TOPK = 16
PEER_QUERY_DIM = 256
PEER_HALF = PEER_QUERY_DIM // 2
TOKEN_CHUNK = 128
N_ADA = 6
EPS = 1e-6

kernel_name = 'hybrid_s5_mla_peer_block'


def rms_norm(x, gain):
    xf = x.astype(jnp.float32)
    y = xf * lax.rsqrt(jnp.mean(xf * xf, axis=-1, keepdims=True) + EPS)
    return (y * gain.astype(jnp.float32)).astype(x.dtype)


def apply_rope(x, positions):
    half = QK_ROPE_DIM // 2
    inv_freq = ROPE_THETA ** (-jnp.arange(half, dtype=jnp.float32) / half)
    ang = positions.astype(jnp.float32)[:, :, None, None] * inv_freq
    cos, sin = jnp.cos(ang), jnp.sin(ang)
    xf = x.astype(jnp.float32)
    x1, x2 = xf[..., :half], xf[..., half:]
    return jnp.concatenate([x1 * cos - x2 * sin, x2 * cos + x1 * sin], axis=-1).astype(x.dtype)


def _complex_scan_combine(e1, e2):
    a1r, a1i, b1r, b1i = e1
    a2r, a2i, b2r, b2i = e2
    ar = a2r * a1r - a2i * a1i
    ai = a2r * a1i + a2i * a1r
    br = a2r * b1r - a2i * b1i + b2r
    bi = a2r * b1i + a2i * b1r + b2i
    return (ar, ai, br, bi)


def s5_group(u, lam_re, lam_im, log_dt, b_re, b_im, c_re, c_im, d_skip, w_glu, b_glu):
    dtype = u.dtype
    bsz, seq, _ = u.shape
    uf = u.astype(jnp.float32).reshape(bsz, seq, N_SSM_GROUPS, SSM_GROUP)
    lr = lam_re.astype(jnp.float32)
    li = lam_im.astype(jnp.float32)
    dt = jnp.exp(log_dt.astype(jnp.float32))[:, None]
    mag = jnp.exp(lr * dt)
    ar = mag * jnp.cos(li * dt)
    ai = mag * jnp.sin(li * dt)
    den = lr * lr + li * li
    nr, ni = ar - 1.0, ai
    coef_r = ((nr * lr + ni * li) / den)[..., None]
    coef_i = ((ni * lr - nr * li) / den)[..., None]
    br_, bi_ = b_re.astype(jnp.float32), b_im.astype(jnp.float32)
    bbar_r = coef_r * br_ - coef_i * bi_
    bbar_i = coef_r * bi_ + coef_i * br_
    bu_r = jnp.einsum('gpc,bsgc->bsgp', bbar_r, uf)
    bu_i = jnp.einsum('gpc,bsgc->bsgp', bbar_i, uf)
    a_r = jnp.broadcast_to(ar[None, None], (1, seq, N_SSM_GROUPS, SSM_STATE))
    a_i = jnp.broadcast_to(ai[None, None], (1, seq, N_SSM_GROUPS, SSM_STATE))
    _, _, xr, xi = lax.associative_scan(_complex_scan_combine, (a_r, a_i, bu_r, bu_i), axis=1)
    y = (jnp.einsum('gcp,bsgp->bsgc', c_re.astype(jnp.float32), xr)
         - jnp.einsum('gcp,bsgp->bsgc', c_im.astype(jnp.float32), xi)
         + d_skip.astype(jnp.float32) * uf)
    y = jax.nn.gelu(y.reshape(bsz, seq, D_SSM))
    y = y * jax.nn.sigmoid(y @ w_glu.astype(jnp.float32) + b_glu.astype(jnp.float32))
    return y.astype(dtype)


def mla_group(c_q, c_kv, k_rope_raw, positions, q_a_norm, w_uq, kv_a_norm, w_ukv, q_norm, k_norm):
    bsz, seq, _ = c_q.shape
    q = (rms_norm(c_q, q_a_norm) @ w_uq).reshape(bsz, seq, N_HEADS, QK_HEAD_DIM)
    kv = (rms_norm(c_kv, kv_a_norm) @ w_ukv).reshape(bsz, seq, N_HEADS, QK_NOPE_DIM + V_HEAD_DIM)
    k_nope, v = kv[..., :QK_NOPE_DIM], kv[..., QK_NOPE_DIM:]
    k_rope = jnp.broadcast_to(k_rope_raw[:, :, None, :], (bsz, seq, N_HEADS, QK_ROPE_DIM))
    k = jnp.concatenate([k_nope, k_rope], axis=-1)
    q = rms_norm(q, q_norm)
    k = rms_norm(k, k_norm)
    q = jnp.concatenate([q[..., :QK_NOPE_DIM], apply_rope(q[..., QK_NOPE_DIM:], positions)], axis=-1)
    k = jnp.concatenate([k[..., :QK_NOPE_DIM], apply_rope(k[..., QK_NOPE_DIM:], positions)], axis=-1)
    q = q.transpose(0, 2, 1, 3)
    k = k.transpose(0, 2, 1, 3)
    v = v.transpose(0, 2, 1, 3)
    scale = QK_HEAD_DIM ** -0.5
    kpos = jnp.arange(seq)

    def attend_block(i):
        start = i * Q_BLOCK
        qb = lax.dynamic_slice_in_dim(q, start, Q_BLOCK, axis=2)
        s = jnp.einsum('bhqd,bhkd->bhqk', qb, k, preferred_element_type=jnp.float32) * scale
        qpos = start + jnp.arange(Q_BLOCK)
        s = jnp.where(kpos[None, :] <= qpos[:, None], s, -jnp.inf)
        p = jax.nn.softmax(s, axis=-1).astype(v.dtype)
        return jnp.einsum('bhqk,bhkd->bhqd', p, v)

    out = lax.map(attend_block, jnp.arange(seq // Q_BLOCK))
    return out.transpose(1, 0, 3, 2, 4).reshape(bsz, seq, N_HEADS * V_HEAD_DIM)


def peer_ffn(h, w_query, sub_keys, expert_down, expert_up):
    bsz, seq, d = h.shape
    chunks = h.reshape((bsz * seq) // TOKEN_CHUNK, TOKEN_CHUNK, d)

    def chunk_fn(hc):
        q = (hc @ w_query).reshape(TOKEN_CHUNK, PEER_HEADS, 2, PEER_HALF)
        s = jnp.einsum('thpd,pkd->thpk', q, sub_keys, preferred_element_type=jnp.float32)
        s1, i1 = lax.top_k(s[:, :, 0], PEER_TOPK)
        s2, i2 = lax.top_k(s[:, :, 1], PEER_TOPK)
        cand = (s1[..., :, None] + s2[..., None, :]).reshape(TOKEN_CHUNK, PEER_HEADS, PEER_TOPK * PEER_TOPK)
        top_s, top_c = lax.top_k(cand, PEER_TOPK)
        e1 = jnp.take_along_axis(i1, top_c // PEER_TOPK, axis=-1)
        e2 = jnp.take_along_axis(i2, top_c % PEER_TOPK, axis=-1)
        idx = e1 * PEER_N_KEYS + e2
        g = jax.nn.softmax(top_s, axis=-1).astype(hc.dtype)
        u = expert_down[idx]
        a = jax.nn.gelu(jnp.einsum('thkd,td->thk', u, hc))
        return jnp.einsum('thk,thkd->td', g * a, expert_up[idx])

    return lax.map(chunk_fn, chunks).reshape(bsz, seq, d)


def hybrid_layer(x, c, positions, w_ada, b_ada, norm_mix, norm_ffn, w_in,
                 lam_re, lam_im, log_dt, b_re, b_im, c_re, c_im, d_skip, w_glu, b_glu,
                 q_a_norm, w_uq, kv_a_norm, w_ukv, q_norm, k_norm,
                 out_norm_ssm, out_norm_attn, w_out,
                 w_query, sub_keys, expert_down, expert_up):
    mod = (jax.nn.silu(c) @ w_ada + b_ada)[:, None, :]
    shift_m, scale_m, gate_m, shift_f, scale_f, gate_f = jnp.split(mod, N_ADA, axis=-1)
    h = rms_norm(x, norm_mix) * (1.0 + scale_m) + shift_m
    proj = h @ w_in
    u_ssm, c_q, c_kv, k_rope_raw = jnp.split(
        proj, [D_SSM, D_SSM + Q_LORA_RANK, D_SSM + Q_LORA_RANK + KV_LORA_RANK], axis=-1)
    y_ssm = s5_group(u_ssm, lam_re, lam_im, log_dt, b_re, b_im, c_re, c_im, d_skip, w_glu, b_glu)
    y_attn = mla_group(c_q, c_kv, k_rope_raw, positions, q_a_norm, w_uq, kv_a_norm, w_ukv, q_norm, k_norm)
    y = jnp.concatenate([rms_norm(y_ssm, out_norm_ssm), rms_norm(y_attn, out_norm_attn)], axis=-1) @ w_out
    x = x + gate_m * y
    h2 = rms_norm(x, norm_ffn) * (1.0 + scale_f) + shift_f
    return x + gate_f * peer_ffn(h2, w_query, sub_keys, expert_down, expert_up)


def setup_inputs(seed: int = 0) -> dict:
    key = jax.random.key(seed)
    ks = jax.random.split(key, 40)
    L = DEPTH

    def nrm(k, shape, scale):
        return jax.random.normal(k, shape, jnp.float32) * scale

    def gain(k, shape):
        return 1.0 + 0.01 * jax.random.normal(k, shape, jnp.float32)

    G, P, C = N_SSM_GROUPS, SSM_STATE, SSM_GROUP
    n_idx = jnp.arange(P, dtype=jnp.float32)
    lam_im = jnp.broadcast_to(math.pi * n_idx, (L, G, P)) + 0.01 * jax.random.normal(ks[8], (L, G, P), jnp.float32)
    log_dt = jax.random.uniform(ks[9], (L, G), jnp.float32, math.log(1e-3), math.log(1e-1))
    positions = jnp.tile(jnp.arange(SEQ, dtype=jnp.int32)[None, :], (BATCH, 1))
    return {
        'x': nrm(ks[0], (BATCH, SEQ, D_MODEL), 1.0),
        'c': nrm(ks[1], (BATCH, D_MODEL), 1.0),
        'positions': positions,
        'w_ada': nrm(ks[2], (L, D_MODEL, N_ADA * D_MODEL), 0.5 * D_MODEL ** -0.5),
        'b_ada': nrm(ks[3], (L, N_ADA * D_MODEL), 0.01),
        'norm_mix': gain(ks[4], (L, D_MODEL)),
        'norm_ffn': gain(ks[5], (L, D_MODEL)),
        'w_in': nrm(ks[6], (L, D_MODEL, D_IN_PROJ), D_MODEL ** -0.5),
        'lam_re': -0.5 + 0.01 * jax.random.normal(ks[7], (L, G, P), jnp.float32),
        'lam_im': lam_im,
        'log_dt': log_dt,
        'b_re': nrm(ks[10], (L, G, P, C), C ** -0.5),
        'b_im': nrm(ks[11], (L, G, P, C), C ** -0.5),
        'c_re': nrm(ks[12], (L, G, C, P), P ** -0.5),
        'c_im': nrm(ks[13], (L, G, C, P), P ** -0.5),
        'd_skip': nrm(ks[14], (L, G, C), 1.0),
        'w_glu': nrm(ks[15], (L, D_SSM, D_SSM), D_SSM ** -0.5),
        'b_glu': nrm(ks[16], (L, D_SSM), 0.01),
        'q_a_norm': gain(ks[17], (L, Q_LORA_RANK)),
        'w_uq': nrm(ks[18], (L, Q_LORA_RANK, N_HEADS * QK_HEAD_DIM), Q_LORA_RANK ** -0.5),
        'kv_a_norm': gain(ks[19], (L, KV_LORA_RANK)),
        'w_ukv': nrm(ks[20], (L, KV_LORA_RANK, N_HEADS * (QK_NOPE_DIM + V_HEAD_DIM)), KV_LORA_RANK ** -0.5),
        'q_norm': gain(ks[21], (L, QK_HEAD_DIM)),
        'k_norm': gain(ks[22], (L, QK_HEAD_DIM)),
        'out_norm_ssm': gain(ks[23], (L, D_SSM)),
        'out_norm_attn': gain(ks[24], (L, D_ATTN)),
        'w_out': nrm(ks[25], (L, D_MIX, D_MODEL), D_MIX ** -0.5),
        'w_query': nrm(ks[26], (L, D_MODEL, PEER_HEADS * PEER_QUERY_DIM), D_MODEL ** -0.5),
        'sub_keys': nrm(ks[27], (L, 2, PEER_N_KEYS, PEER_HALF), PEER_HALF ** -0.5),
        'expert_down': nrm(ks[28], (L, PEER_N_EXPERTS, D_MODEL), D_MODEL ** -0.5),
        'expert_up': nrm(ks[29], (L, PEER_N_EXPERTS, D_MODEL), 0.5),
    }


def reference(x, c, positions, w_ada, b_ada, norm_mix, norm_ffn, w_in,
              lam_re, lam_im, log_dt, b_re, b_im, c_re, c_im, d_skip, w_glu, b_glu,
              q_a_norm, w_uq, kv_a_norm, w_ukv, q_norm, k_norm,
              out_norm_ssm, out_norm_attn, w_out,
              w_query, sub_keys, expert_down, expert_up):
    for l in range(DEPTH):
        x = hybrid_layer(x, c, positions, w_ada[l], b_ada[l], norm_mix[l], norm_ffn[l], w_in[l],
                         lam_re[l], lam_im[l], log_dt[l], b_re[l], b_im[l], c_re[l], c_im[l],
                         d_skip[l], w_glu[l], b_glu[l],
                         q_a_norm[l], w_uq[l], kv_a_norm[l], w_ukv[l], q_norm[l], k_norm[l],
                         out_norm_ssm[l], out_norm_attn[l], w_out[l],
                         w_query[l], sub_keys[l], expert_down[l], expert_up[l])
    return x
```

```python
import functools
import math

import numpy as np
import jax
import jax.numpy as jnp
from jax import lax
from jax.experimental import pallas as pl
from jax.experimental.pallas import tpu as pltpu

F32 = jnp.float32
BF16 = jnp.bfloat16

D_MODEL = 1024
D_SSM = 512
SSM_GROUP = 16
N_SSM_GROUPS = 32
SSM_STATE = 64
N_STATE = N_SSM_GROUPS * SSM_STATE
N_HEADS = 8
QK_NOPE = 64
QK_ROPE = 32
QK_HEAD = 96
V_HEAD = 64
HEAD_PAD = 128
Q_LORA = 256
KV_LORA = 128
ROPE_THETA = 10000.0
PEER_HEADS = 8
PEER_KEYS = 128
PEER_TOPK = 16
PEER_HALF = 128
N_SEL = PEER_HEADS * PEER_TOPK
N_ADA = 6
EPS = 1e-6
GELU_C = math.sqrt(2.0 / math.pi)

VMEM_LIMIT = 48 * 1024 * 1024


def _dot(a, b):
    return jnp.dot(a, b, preferred_element_type=F32)


def _dot_nt(a, b):
    return lax.dot_general(a, b, (((1,), (1,)), ((), ())), preferred_element_type=F32)


def _dot_split(a, sel):
    hi = a.astype(BF16)
    lo = (a - hi.astype(F32)).astype(BF16)
    return _dot(hi, sel) + _dot(lo, sel)


def _rms(x):
    return x * lax.rsqrt(jnp.mean(x * x, axis=-1, keepdims=True) + EPS)


def _gelu(x):
    return 0.5 * x * (1.0 + jnp.tanh(GELU_C * (x + 0.044715 * x * x * x)))


def _params(*sem):
    return pltpu.CompilerParams(dimension_semantics=sem, vmem_limit_bytes=VMEM_LIMIT)


def _const_spec(shape):
    nd = len(shape)
    return pl.BlockSpec(shape, lambda *_: (0,) * nd)


def _ada_kernel(c_ref, w_ref, b_ref, o_ref):
    c = c_ref[...]
    s = c * jax.nn.sigmoid(c)
    o_ref[...] = _dot_split2(s, w_ref[...]) + b_ref[...]


def _dot_split2(a, w):
    ah = a.astype(BF16)
    al = (a - ah.astype(F32)).astype(BF16)
    wh = w.astype(BF16)
    wl = (w - wh.astype(F32)).astype(BF16)
    return _dot(ah, wh) + (_dot(ah, wl) + _dot(al, wh))


def _ada(c, w_ada, b_ada):
    bsz = c.shape[0]
    n = w_ada.shape[1]
    blk = D_MODEL
    return pl.pallas_call(
        _ada_kernel,
        grid=(n // blk,),
        in_specs=[_const_spec((bsz, D_MODEL)),
                  pl.BlockSpec((D_MODEL, blk), lambda j: (0, j)),
                  pl.BlockSpec((1, blk), lambda j: (0, j))],
        out_specs=pl.BlockSpec((bsz, blk), lambda j: (0, j)),
        out_shape=jax.ShapeDtypeStruct((bsz, n), F32),
        compiler_params=_params("arbitrary"),
        name="ada",
    )(c, w_ada, b_ada.reshape(1, n))


def _rope_kernel(pos_ref, freq_ref, cos_ref, sin_ref):
    ang = pos_ref[...].astype(F32) * freq_ref[...]
    cos_ref[...] = jnp.cos(ang)
    sin_ref[...] = jnp.sin(ang)


def _rope_tables(positions):
    half = QK_ROPE // 2
    t = positions.size
    rows = t * half // 128
    pos_rep = jnp.repeat(positions.reshape(-1), half).reshape(rows, 128)
    inv_freq = ROPE_THETA ** (-jnp.arange(half, dtype=F32) / half)
    freq_row = jnp.tile(inv_freq, 128 // half).reshape(1, 128)
    blk = min(rows, 512)
    cos_d, sin_d = pl.pallas_call(
        _rope_kernel,
        grid=(rows // blk,),
        in_specs=[pl.BlockSpec((blk, 128), lambda i: (i, 0)), _const_spec((1, 128))],
        out_specs=[pl.BlockSpec((blk, 128), lambda i: (i, 0))] * 2,
        out_shape=[jax.ShapeDtypeStruct((rows, 128), F32)] * 2,
        compiler_params=_params("arbitrary"),
        name="rope",
    )(pos_rep, freq_row)
    return jnp.concatenate(
        [cos_d.reshape(t, half), sin_d.reshape(t, half), jnp.zeros((t, 128 - 2 * half), F32)], axis=1)


def _s5par_kernel(lr_ref, li_ref, ldt_ref, bre_ref, bim_ref, ar_ref, ai_ref, bbr_ref, bbi_ref):
    lr = lr_ref[...]
    li = li_ref[...]
    dt = jnp.exp(ldt_ref[...])
    mag = jnp.exp(lr * dt)
    ar = mag * jnp.cos(li * dt)
    ai = mag * jnp.sin(li * dt)
    den = lr * lr + li * li
    nr = ar - 1.0
    ni = ai
    coef_r = (nr * lr + ni * li) / den
    coef_i = (ni * lr - nr * li) / den
    ar_ref[...] = ar
    ai_ref[...] = ai
    bre = bre_ref[...]
    bim = bim_ref[...]
    cr = coef_r[:, None, :]
    ci = coef_i[:, None, :]
    bbr_ref[...] = cr * bre - ci * bim
    bbi_ref[...] = cr * bim + ci * bre


def _s5_params(lam_re, lam_im, log_dt, b_re, b_im):
    g, p = lam_re.shape
    c = b_re.shape[-1]
    bre_t = jnp.transpose(b_re, (0, 2, 1))
    bim_t = jnp.transpose(b_im, (0, 2, 1))
    return pl.pallas_call(
        _s5par_kernel,
        out_shape=[jax.ShapeDtypeStruct((g, p), F32), jax.ShapeDtypeStruct((g, p), F32),
                   jax.ShapeDtypeStruct((g, c, p), F32), jax.ShapeDtypeStruct((g, c, p), F32)],
        name="s5par",
    )(lam_re, lam_im, log_dt.reshape(g, 1), bre_t, bim_t)


def _block_diag(blocks):
    g, r, c = blocks.shape
    eye = jnp.eye(g, dtype=blocks.dtype)
    return (blocks[:, :, None, :] * eye[:, None, :, None]).reshape(g * r, g * c)


@functools.lru_cache(maxsize=None)
def _layout_constants():
    hp, nh = HEAD_PAD, N_HEADS
    width = nh * hp
    half = QK_ROPE // 2
    place = np.zeros((128, width), np.float32)
    hsel = np.zeros((width, 128), np.float32)
    rot = np.zeros((width, width), np.float32)
    ecos = np.zeros((128, width), np.float32)
    esin = np.zeros((128, width), np.float32)
    for h in range(nh):
        base = h * hp
        hsel[base:base + QK_HEAD, h] = 1.0
        for j in range(QK_ROPE):
            place[j, base + QK_NOPE + j] = 1.0
        for j in range(half):
            c1 = base + QK_NOPE + j
            c2 = c1 + half
            rot[c2, c1] = -1.0
            rot[c1, c2] = 1.0
            ecos[j, c1] = 1.0
            ecos[j, c2] = 1.0
            esin[half + j, c1] = 1.0
            esin[half + j, c2] = 1.0
    ones_nope = (ecos.sum(axis=0, keepdims=True) == 0).astype(np.float32)
    return place, hsel, rot, ecos, esin, ones_nope


def _pad_heads(w, head_dim):
    k = w.shape[0]
    w = w.reshape(k, N_HEADS, head_dim)
    w = jnp.pad(w, ((0, 0), (0, 0), (0, HEAD_PAD - head_dim)))
    return w.reshape(k, N_HEADS * HEAD_PAD)


def _inproj_kernel(x_ref, scale_ref, shift_ref, nmix_ref, win_ref, qan_ref, wuq_ref, kvan_ref,
                   wk_ref, wv_ref, place_ref, hsel_ref, hselt_ref, qg_ref, kg_ref, rot_ref,
                   ecos_ref, esin_ref, nope_ref, cs_ref,
                   u_ref, q_ref, k_ref, v_ref):
    x = x_ref[...]
    h = _rms(x) * nmix_ref[...]
    h = h * (1.0 + scale_ref[0]) + shift_ref[0]
    proj = _dot(h.astype(BF16), win_ref[...])
    u_ref[...] = proj[:, :D_SSM]
    o1 = D_SSM + Q_LORA
    o2 = o1 + KV_LORA
    cq = _rms(proj[:, D_SSM:o1]) * qan_ref[...]
    ckv = (_rms(proj[:, o1:o2]) * kvan_ref[...]).astype(BF16)
    kr = proj[:, o2:]
    q = _dot(cq.astype(BF16), wuq_ref[...])
    k = _dot(ckv, wk_ref[...]) + _dot_split(kr, place_ref[...])
    v_ref[...] = _dot(ckv, wv_ref[...]).astype(v_ref.dtype)

    cs = cs_ref[...]
    cos = _dot_split(cs, ecos_ref[...]) + nope_ref[...]
    sin = _dot_split(cs, esin_ref[...])

    def head_norm_rope(z, gain):
        ssq = _dot_split(z * z, hsel_ref[...])
        r = lax.rsqrt(ssq * (1.0 / QK_HEAD) + EPS)
        zn = z * _dot_split(r, hselt_ref[...]) * gain
        return zn * cos + _dot(zn.astype(BF16), rot_ref[...]) * sin

    qr = head_norm_rope(q, qg_ref[...]) * (QK_HEAD ** -0.5)
    q_ref[...] = qr.astype(q_ref.dtype)
    k_ref[...] = head_norm_rope(k, kg_ref[...]).astype(k_ref.dtype)


def _inproj(x2, scale_m, shift_m, norm_mix, w_in, q_a_norm, w_uq, kv_a_norm, w_ukv, q_norm, k_norm,
            cs, bsz, seq, tm):
    t = bsz * seq
    nsb = seq // tm
    place, hsel, rot, ecos, esin, ones_nope = _layout_constants()
    width = N_HEADS * HEAD_PAD
    win = jnp.pad(w_in, ((0, 0), (0, D_MODEL - w_in.shape[1]))).astype(BF16)
    wuq = _pad_heads(w_uq, QK_HEAD).astype(BF16)
    wkv = w_ukv.reshape(KV_LORA, N_HEADS, QK_NOPE + V_HEAD)
    wk = _pad_heads(wkv[:, :, :QK_NOPE].reshape(KV_LORA, N_HEADS * QK_NOPE), QK_NOPE).astype(BF16)
    wv = wkv[:, :, QK_NOPE:].reshape(KV_LORA, N_HEADS * V_HEAD).astype(BF16)
    qg = _pad_heads(jnp.tile(q_norm, N_HEADS).reshape(1, -1), QK_HEAD)
    kg = _pad_heads(jnp.tile(k_norm, N_HEADS).reshape(1, -1), QK_HEAD)
    consts = [
        norm_mix.reshape(1, D_MODEL), win, q_a_norm.reshape(1, Q_LORA), wuq,
        kv_a_norm.reshape(1, KV_LORA), wk, wv,
        jnp.asarray(place, BF16), jnp.asarray(hsel, BF16), jnp.asarray(hsel.T.copy(), BF16), qg, kg,
        jnp.asarray(rot, BF16), jnp.asarray(ecos, BF16), jnp.asarray(esin, BF16),
        jnp.asarray(ones_nope, F32),
    ]
    mod_spec = pl.BlockSpec((1, 1, D_MODEL), lambda i: (i // nsb, 0, 0))
    in_specs = ([pl.BlockSpec((tm, D_MODEL), lambda i: (i, 0)), mod_spec, mod_spec]
                + [_const_spec(a.shape) for a in consts]
                + [pl.BlockSpec((tm, 128), lambda i: (i, 0))])
    out_specs = [
        pl.BlockSpec((tm, D_SSM), lambda i: (i % nsb, i // nsb)),
        pl.BlockSpec((tm, width), lambda i: (i, 0)),
        pl.BlockSpec((tm, width), lambda i: (i, 0)),
        pl.BlockSpec((tm, N_HEADS * V_HEAD), lambda i: (i, 0)),
    ]
    out_shape = [
        jax.ShapeDtypeStruct((seq, bsz * D_SSM), F32),
        jax.ShapeDtypeStruct((t, width), BF16),
        jax.ShapeDtypeStruct((t, width), BF16),
        jax.ShapeDtypeStruct((t, N_HEADS * V_HEAD), BF16),
    ]
    return pl.pallas_call(
        _inproj_kernel, grid=(t // tm,), in_specs=in_specs, out_specs=out_specs, out_shape=out_shape,
        compiler_params=_params("arbitrary"), name="inproj",
    )(x2, scale_m, shift_m, *consts, cs)


def _attn_kernel(q_ref, k_ref, v_ref, o_ref, *, tq):
    qi = pl.program_id(2)
    q0 = q_ref[:, :HEAD_PAD]
    q1 = q_ref[:, HEAD_PAD:]
    lane = lax.broadcasted_iota(jnp.int32, (1, 2 * V_HEAD), 1)
    first = lane < V_HEAD

    def block(j, carry, masked):
        m0, l0, m1, l1, acc = carry
        r0 = pl.multiple_of(j * tq, tq)
        kb = k_ref[pl.ds(r0, tq), :]
        vb = v_ref[pl.ds(r0, tq), :]
        s0 = _dot_nt(q0, kb[:, :HEAD_PAD])
        s1 = _dot_nt(q1, kb[:, HEAD_PAD:])
        if masked:
            row = lax.broadcasted_iota(jnp.int32, (tq, tq), 0)
            col = lax.broadcasted_iota(jnp.int32, (tq, tq), 1)
            keep = col <= row
            s0 = jnp.where(keep, s0, -jnp.inf)
            s1 = jnp.where(keep, s1, -jnp.inf)
        n0 = jnp.maximum(m0, jnp.max(s0, axis=-1, keepdims=True))
        n1 = jnp.maximum(m1, jnp.max(s1, axis=-1, keepdims=True))
        a0 = jnp.exp(m0 - n0)
        a1 = jnp.exp(m1 - n1)
        p0 = jnp.exp(s0 - n0)
        p1 = jnp.exp(s1 - n1)
        l0 = a0 * l0 + jnp.sum(p0, axis=-1, keepdims=True)
        l1 = a1 * l1 + jnp.sum(p1, axis=-1, keepdims=True)
        zero = jnp.zeros_like(vb)
        pv = (_dot(p0.astype(BF16), jnp.where(first, vb, zero))
              + _dot(p1.astype(BF16), jnp.where(first, zero, vb)))
        acc = acc * jnp.where(first, a0, a1) + pv
        return n0, l0, n1, l1, acc

    neg = jnp.full((tq, 1), -jnp.inf, F32)
    zl = jnp.zeros((tq, 1), F32)
    carry = (neg, zl, neg, zl, jnp.zeros((tq, 2 * V_HEAD), F32))
    carry = lax.fori_loop(0, qi, lambda j, c: block(j, c, False), carry)
    _, l0, _, l1, acc = block(qi, carry, True)
    o_ref[...] = (acc * jnp.where(first, 1.0 / l0, 1.0 / l1)).astype(o_ref.dtype)


def _attention(q, k, v, bsz, seq, tq):
    nq = seq // tq
    t = bsz * seq
    return pl.pallas_call(
        functools.partial(_attn_kernel, tq=tq),
        grid=(bsz, N_HEADS // 2, nq),
        in_specs=[pl.BlockSpec((tq, 2 * HEAD_PAD), lambda b, h, i: (b * nq + i, h)),
                  pl.BlockSpec((seq, 2 * HEAD_PAD), lambda b, h, i: (b, h)),
                  pl.BlockSpec((seq, 2 * V_HEAD), lambda b, h, i: (b, h))],
        out_specs=pl.BlockSpec((tq, 2 * V_HEAD), lambda b, h, i: (b * nq + i, h)),
        out_shape=jax.ShapeDtypeStruct((t, N_HEADS * V_HEAD), BF16),
        compiler_params=_params("arbitrary", "arbitrary", "arbitrary"),
        name="attn",
    )(q, k, v)


def _s5_kernel(u_ref, bd_ref, ar_ref, ai_ref, cd_ref, dsk_ref, wglu_ref, bglu_ref, gain_ref,
               y_ref, bu_ref, st_ref, *, lc, bsz, cb):
    @pl.when(pl.program_id(0) == 0)
    def _():
        st_ref[...] = jnp.zeros_like(st_ref)

    u = u_ref[...]
    bu_ref[...] = _dot(u.astype(BF16), bd_ref[...])

    for c0 in range(0, N_STATE, cb):
        ar = ar_ref[:, c0:c0 + cb]
        ai = ai_ref[:, c0:c0 + cb]

        def step(t, carry, c0=c0, ar=ar, ai=ai):
            xr, xi = carry
            r0 = pl.multiple_of(t * bsz, bsz)
            bur = bu_ref[pl.ds(r0, bsz), c0:c0 + cb]
            bui = bu_ref[pl.ds(r0, bsz), N_STATE + c0:N_STATE + c0 + cb]
            nxr = ar * xr - ai * xi + bur
            nxi = ar * xi + ai * xr + bui
            bu_ref[pl.ds(r0, bsz), c0:c0 + cb] = nxr
            bu_ref[pl.ds(r0, bsz), N_STATE + c0:N_STATE + c0 + cb] = nxi
            return nxr, nxi

        xr0 = st_ref[:, c0:c0 + cb]
        xi0 = st_ref[:, N_STATE + c0:N_STATE + c0 + cb]
        xr, xi = lax.fori_loop(0, lc, step, (xr0, xi0))
        st_ref[:, c0:c0 + cb] = xr
        st_ref[:, N_STATE + c0:N_STATE + c0 + cb] = xi

    y = _dot(bu_ref[...].astype(BF16), cd_ref[...]) + dsk_ref[...] * u
    y = _gelu(y)
    y = y * jax.nn.sigmoid(_dot(y.astype(BF16), wglu_ref[...]) + bglu_ref[...])
    y_ref[...] = (_rms(y) * gain_ref[...]).astype(y_ref.dtype)


def _s5(u_tm, bd, ar, ai, cd, d_skip, w_glu, b_glu, gain, bsz, seq, lc):
    rows = lc * bsz
    consts = [bd, ar, ai, cd, d_skip.reshape(1, D_SSM), w_glu.astype(BF16), b_glu.reshape(1, D_SSM),
              gain.reshape(1, D_SSM)]
    return pl.pallas_call(
        functools.partial(_s5_kernel, lc=lc, bsz=bsz, cb=512),
        grid=(seq // lc,),
        in_specs=[pl.BlockSpec((rows, D_SSM), lambda j: (j, 0))] + [_const_spec(a.shape) for a in consts],
        out_specs=pl.BlockSpec((rows, D_SSM), lambda j: (j, 0)),
        out_shape=jax.ShapeDtypeStruct((seq * bsz, D_SSM), BF16),
        scratch_shapes=[pltpu.VMEM((rows, 2 * N_STATE), F32), pltpu.VMEM((bsz, 2 * N_STATE), F32)],
        compiler_params=_params("arbitrary"),
        name="s5",
    )(u_tm, *consts)


def _outproj_kernel(ys_ref, ya_ref, x_ref, gate_ref, scale_ref, shift_ref, ga_ref, wos_ref, woa_ref,
                    nffn_ref, wq_ref, x1_ref, h2_ref, qp_ref):
    ya = _rms(ya_ref[...].astype(F32)) * ga_ref[...]
    y = _dot(ys_ref[...], wos_ref[...]) + _dot(ya.astype(BF16), woa_ref[...])
    x1 = x_ref[...] + gate_ref[0] * y
    x1_ref[...] = x1
    h2 = _rms(x1) * nffn_ref[...]
    h2 = h2 * (1.0 + scale_ref[0]) + shift_ref[0]
    h2_ref[...] = h2
    qp_ref[...] = _dot(h2.astype(BF16), wq_ref[...]).astype(qp_ref.dtype)


def _outproj(ys_tm, ya, x2, gate_m, scale_f, shift_f, out_norm_attn, w_out, norm_ffn, w_query,
             bsz, seq, tm):
    t = bsz * seq
    nsb = seq // tm
    nq = w_query.shape[1]
    wo = w_out.astype(BF16)
    consts = [out_norm_attn.reshape(1, -1), wo[:D_SSM], wo[D_SSM:], norm_ffn.reshape(1, D_MODEL),
              w_query.astype(BF16)]
    mod_spec = pl.BlockSpec((1, 1, D_MODEL), lambda i: (i // nsb, 0, 0))
    row_spec = pl.BlockSpec((tm, D_MODEL), lambda i: (i, 0))
    return pl.pallas_call(
        _outproj_kernel,
        grid=(t // tm,),
        in_specs=[pl.BlockSpec((tm, D_SSM), lambda i: (i % nsb, i // nsb)),
                  pl.BlockSpec((tm, D_SSM), lambda i: (i, 0)),
                  row_spec, mod_spec, mod_spec, mod_spec] + [_const_spec(a.shape) for a in consts],
        out_specs=[row_spec, row_spec, pl.BlockSpec((tm, nq), lambda i: (i, 0))],
        out_shape=[jax.ShapeDtypeStruct((t, D_MODEL), F32), jax.ShapeDtypeStruct((t, D_MODEL), F32),
                   jax.ShapeDtypeStruct((t, nq), BF16)],
        compiler_params=_params("arbitrary"),
        name="outproj",
    )(ys_tm, ya, x2, gate_m, scale_f, shift_f, *consts)


def _topk_rows(s, k):
    n_rows = s.shape[0]
    iota = lax.broadcasted_iota(jnp.int32, s.shape, 0)
    vals, idxs = [], []
    for _ in range(k):
        m = jnp.max(s, axis=0, keepdims=True)
        ix = jnp.min(jnp.where(s == m, iota, n_rows), axis=0, keepdims=True)
        vals.append(m)
        idxs.append(ix)
        s = jnp.where(iota == ix, -jnp.inf, s)
    return jnp.concatenate(vals, axis=0), jnp.concatenate(idxs, axis=0)


def _pick_rows(sel, table):
    out = jnp.zeros(sel.shape, table.dtype)
    for a in range(table.shape[0]):
        out = jnp.where(sel == a, table[a:a + 1, :], out)
    return out


def _topk_kernel(q_ref, keys_ref, idx_ref, g_ref):
    k = PEER_TOPK
    s1 = _dot_nt(keys_ref[0], q_ref[:, :PEER_HALF])
    s2 = _dot_nt(keys_ref[1], q_ref[:, PEER_HALF:])
    v1, i1 = _topk_rows(s1, k)
    v2, i2 = _topk_rows(s2, k)
    cand = jnp.concatenate([v1[a:a + 1, :] + v2 for a in range(k)], axis=0)
    top_s, top_c = _topk_rows(cand, k)
    e1 = _pick_rows(top_c >> 4, i1)
    e2 = _pick_rows(top_c & (k - 1), i2)
    idx_ref[0] = e1 * PEER_KEYS + e2
    e = jnp.exp(top_s - top_s[0:1, :])
    g_ref[0] = e / jnp.sum(e, axis=0, keepdims=True)


def _topk(qp, sub_keys, tm):
    t = qp.shape[0]
    return pl.pallas_call(
        _topk_kernel,
        grid=(t // tm, PEER_HEADS),
        in_specs=[pl.BlockSpec((tm, 2 * PEER_HALF), lambda i, h: (i, h)),
                  _const_spec(sub_keys.shape)],
        out_specs=[pl.BlockSpec((1, PEER_TOPK, tm), lambda i, h: (h, 0, i))] * 2,
        out_shape=[jax.ShapeDtypeStruct((PEER_HEADS, PEER_TOPK, t), jnp.int32),
                   jax.ShapeDtypeStruct((PEER_HEADS, PEER_TOPK, t), F32)],
        compiler_params=_params("arbitrary", "arbitrary"),
        name="topk",
    )(qp, sub_keys.astype(BF16))


def _pack_table(w):
    e, d = w.shape
    bits = lax.bitcast_convert_type(w.astype(BF16), jnp.uint16).astype(jnp.uint32)
    words = bits[:, :d // 2] | (bits[:, d // 2:] << 16)
    return lax.bitcast_convert_type(words, jnp.int32).reshape(e, d // 256, 128)


def _unpack(words):
    lo = pltpu.bitcast(words << 16, F32)
    hi = pltpu.bitcast(words & jnp.int32(-65536), F32)
    return lo, hi


def _peer_dn_kernel(idx_ref, h_ref, tbl_ref, a_ref, prod_ref, *, tt):
    ones = jnp.ones((8, 128), BF16)

    def token(t, _):
        hl = h_ref[t, 0:4, :]
        hh = h_ref[t, 4:8, :]
        base = t * N_SEL
        for k in range(N_SEL):
            lo, hi = _unpack(tbl_ref[idx_ref[base + k]])
            prod_ref[k] = lo * hl + hi * hh
        part = jnp.sum(prod_ref[...], axis=1)
        a_ref[t] = _dot_nt_split(ones, part)[0:1, :]
        return 0

    lax.fori_loop(0, tt, token, 0)


def _dot_nt_split(sel, a):
    hi = a.astype(BF16)
    lo = (a - hi.astype(F32)).astype(BF16)
    return _dot_nt(sel, hi) + _dot_nt(sel, lo)


def _peer_dn(idx_flat, h3, tbl, tt):
    t = h3.shape[0]
    return pl.pallas_call(
        functools.partial(_peer_dn_kernel, tt=tt),
        grid=(t // tt,),
        in_specs=[pl.BlockSpec((tt * N_SEL,), lambda i: (i,), memory_space=pltpu.SMEM),
                  pl.BlockSpec((tt, 8, 128), lambda i: (i, 0, 0)),
                  pl.BlockSpec(memory_space=pltpu.VMEM)],
        out_specs=pl.BlockSpec((tt, 1, N_SEL), lambda i: (i, 0, 0)),
        out_shape=jax.ShapeDtypeStruct((t, 1, N_SEL), F32),
        scratch_shapes=[pltpu.VMEM((N_SEL, 4, 128), F32)],
        compiler_params=_params("arbitrary"),
        name="peer_dn",
    )(idx_flat, h3, tbl)


def _gate_kernel(a_ref, g_ref, w_ref):
    w_ref[...] = g_ref[...] * _gelu(a_ref[...])


def _gate(a, g):
    t = a.shape[0]
    blk = min(t, 2048)
    spec = pl.BlockSpec((blk, N_SEL), lambda i: (i, 0))
    return pl.pallas_call(
        _gate_kernel, grid=(t // blk,), in_specs=[spec, spec], out_specs=spec,
        out_shape=jax.ShapeDtypeStruct((t, N_SEL), F32),
        compiler_params=_params("arbitrary"), name="gate",
    )(a, g)


def _peer_up_kernel(idx_ref, w_ref, x1_ref, gate_ref, tbl_ref, o_ref, *, tt):
    gate = gate_ref[0]

    def token(t, _):
        base = t * N_SEL
        acc_lo = jnp.zeros((4, 128), F32)
        acc_hi = jnp.zeros((4, 128), F32)
        for k in range(N_SEL):
            lo, hi = _unpack(tbl_ref[idx_ref[base + k]])
            wk = w_ref[base + k]
            acc_lo = acc_lo + wk * lo
            acc_hi = acc_hi + wk * hi
        peer = jnp.concatenate([acc_lo, acc_hi], axis=0)
        o_ref[t] = x1_ref[t] + gate * peer
        return 0

    lax.fori_loop(0, tt, token, 0)


def _peer_up(idx_flat, w_flat, x13, gate3, tbl, seq, tt):
    t = x13.shape[0]
    nsb = seq // tt
    smem = lambda: pl.BlockSpec((tt * N_SEL,), lambda i: (i,), memory_space=pltpu.SMEM)
    row = pl.BlockSpec((tt, 8, 128), lambda i: (i, 0, 0))
    return pl.pallas_call(
        functools.partial(_peer_up_kernel, tt=tt),
        grid=(t // tt,),
        in_specs=[smem(), smem(), row, pl.BlockSpec((1, 8, 128), lambda i: (i // nsb, 0, 0)),
                  pl.BlockSpec(memory_space=pltpu.VMEM)],
        out_specs=row,
        out_shape=jax.ShapeDtypeStruct((t, 8, 128), F32),
        compiler_params=_params("arbitrary"),
        name="peer_up",
    )(idx_flat, w_flat, x13, gate3, tbl)


def _layer(x, c, positions, w_ada, b_ada, norm_mix, norm_ffn, w_in, lam_re, lam_im, log_dt, b_re, b_im,
           c_re, c_im, d_skip, w_glu, b_glu, q_a_norm, w_uq, kv_a_norm, w_ukv, q_norm, k_norm,
           out_norm_ssm, out_norm_attn, w_out, w_query, sub_keys, expert_down, expert_up):
    bsz, seq, _ = x.shape
    t = bsz * seq
    tm = min(256, seq)
    x2 = x.reshape(t, D_MODEL)

    mod = _ada(c, w_ada, b_ada)
    shift_m, scale_m, gate_m, shift_f, scale_f, gate_f = [
        m.reshape(bsz, 1, D_MODEL) for m in jnp.split(mod, N_ADA, axis=-1)]

    cs = _rope_tables(positions)
    u_tm, q, k, v = _inproj(x2, scale_m, shift_m, norm_mix, w_in, q_a_norm, w_uq, kv_a_norm, w_ukv,
                            q_norm, k_norm, cs, bsz, seq, tm)
    ya = _attention(q, k, v, bsz, seq, tm)

    ar, ai, bbr, bbi = _s5_params(lam_re, lam_im, log_dt, b_re, b_im)
    bd = jnp.concatenate([_block_diag(bbr), _block_diag(bbi)], axis=1).astype(BF16)
    cd = jnp.concatenate([_block_diag(jnp.transpose(c_re, (0, 2, 1))),
                          _block_diag(jnp.transpose(-c_im, (0, 2, 1)))], axis=0).astype(BF16)
    ys_tm = _s5(u_tm.reshape(seq * bsz, D_SSM), bd, ar.reshape(1, N_STATE), ai.reshape(1, N_STATE), cd,
                d_skip, w_glu, b_glu, out_norm_ssm, bsz, seq, min(32, seq))

    x1, h2, qp = _outproj(ys_tm.reshape(seq, bsz * D_SSM), ya, x2, gate_m, scale_f, shift_f,
                          out_norm_attn, w_out, norm_ffn, w_query, bsz, seq, tm)

    idx_t, g_t = _topk(qp, sub_keys, tm)
    idx = jnp.transpose(idx_t.reshape(N_SEL, t))
    g = jnp.transpose(g_t.reshape(N_SEL, t))
    idx_flat = idx.reshape(t * N_SEL)

    tt = min(128, seq)
    a = _peer_dn(idx_flat, h2.reshape(t, 8, 128), _pack_table(expert_down), tt)
    wgt = _gate(a.reshape(t, N_SEL), g)
    out = _peer_up(idx_flat, wgt.reshape(t * N_SEL), x1.reshape(t, 8, 128),
                   gate_f.reshape(bsz, 8, 128), _pack_table(expert_up), seq, tt)
    return out.reshape(bsz, seq, D_MODEL)


def kernel(x, c, positions, w_ada, b_ada, norm_mix, norm_ffn, w_in, lam_re, lam_im, log_dt, b_re, b_im, c_re, c_im, d_skip, w_glu, b_glu, q_a_norm, w_uq, kv_a_norm, w_ukv, q_norm, k_norm, out_norm_ssm, out_norm_attn, w_out, w_query, sub_keys, expert_down, expert_up):
    for l in range(w_ada.shape[0]):
        x = _layer(x, c, positions, w_ada[l], b_ada[l], norm_mix[l], norm_ffn[l], w_in[l],
                   lam_re[l], lam_im[l], log_dt[l], b_re[l], b_im[l], c_re[l], c_im[l],
                   d_skip[l], w_glu[l], b_glu[l], q_a_norm[l], w_uq[l], kv_a_norm[l], w_ukv[l],
                   q_norm[l], k_norm[l], out_norm_ssm[l], out_norm_attn[l], w_out[l],
                   w_query[l], sub_keys[l], expert_down[l], expert_up[l])
    return x
```

```python
import functools
import math

import numpy as np
import jax
import jax.numpy as jnp
from jax import lax
from jax.experimental import pallas as pl
from jax.experimental.pallas import tpu as pltpu

F32 = jnp.float32
BF16 = jnp.bfloat16

D_MODEL = 1024
D_SSM = 512
SSM_GROUP = 16
N_SSM_GROUPS = 32
SSM_STATE = 64
N_STATE = N_SSM_GROUPS * SSM_STATE
N_HEADS = 8
QK_NOPE = 64
QK_ROPE = 32
QK_HEAD = 96
V_HEAD = 64
HEAD_PAD = 128
Q_LORA = 256
KV_LORA = 128
ROPE_THETA = 10000.0
PEER_HEADS = 8
PEER_KEYS = 128
PEER_TOPK = 16
PEER_HALF = 128
N_SEL = PEER_HEADS * PEER_TOPK
N_ADA = 6
EPS = 1e-6
GELU_C = math.sqrt(2.0 / math.pi)

VMEM_LIMIT = 48 * 1024 * 1024


def _dot(a, b):
    return jnp.dot(a, b, preferred_element_type=F32)


def _dot_nt(a, b):
    return lax.dot_general(a, b, (((1,), (1,)), ((), ())), preferred_element_type=F32)


def _dot_split(a, sel):
    hi = a.astype(BF16)
    lo = (a - hi.astype(F32)).astype(BF16)
    return _dot(hi, sel) + _dot(lo, sel)


def _rms(x):
    return x * lax.rsqrt(jnp.mean(x * x, axis=-1, keepdims=True) + EPS)


def _gelu(x):
    return 0.5 * x * (1.0 + jnp.tanh(GELU_C * (x + 0.044715 * x * x * x)))


def _params(*sem):
    return pltpu.CompilerParams(dimension_semantics=sem, vmem_limit_bytes=VMEM_LIMIT)


def _const_spec(shape):
    nd = len(shape)
    return pl.BlockSpec(shape, lambda *_: (0,) * nd)


def _ada_kernel(c_ref, w_ref, b_ref, o_ref):
    c = c_ref[...]
    s = c * jax.nn.sigmoid(c)
    o_ref[...] = _dot_split2(s, w_ref[...]) + b_ref[...]


def _dot_split2(a, w):
    ah = a.astype(BF16)
    al = (a - ah.astype(F32)).astype(BF16)
    wh = w.astype(BF16)
    wl = (w - wh.astype(F32)).astype(BF16)
    return _dot(ah, wh) + (_dot(ah, wl) + _dot(al, wh))


def _ada(c, w_ada, b_ada):
    bsz = c.shape[0]
    n = w_ada.shape[1]
    blk = D_MODEL
    return pl.pallas_call(
        _ada_kernel,
        grid=(n // blk,),
        in_specs=[_const_spec((bsz, D_MODEL)),
                  pl.BlockSpec((D_MODEL, blk), lambda j: (0, j)),
                  pl.BlockSpec((1, blk), lambda j: (0, j))],
        out_specs=pl.BlockSpec((bsz, blk), lambda j: (0, j)),
        out_shape=jax.ShapeDtypeStruct((bsz, n), F32),
        compiler_params=_params("arbitrary"),
        name="ada",
    )(c, w_ada, b_ada.reshape(1, n))


def _rope_kernel(pos_ref, freq_ref, cos_ref, sin_ref):
    ang = pos_ref[...].astype(F32) * freq_ref[...]
    cos_ref[...] = jnp.cos(ang)
    sin_ref[...] = jnp.sin(ang)


def _rope_tables(positions):
    half = QK_ROPE // 2
    t = positions.size
    rows = t * half // 128
    pos_rep = jnp.repeat(positions.reshape(-1), half).reshape(rows, 128)
    inv_freq = ROPE_THETA ** (-jnp.arange(half, dtype=F32) / half)
    freq_row = jnp.tile(inv_freq, 128 // half).reshape(1, 128)
    blk = min(rows, 512)
    cos_d, sin_d = pl.pallas_call(
        _rope_kernel,
        grid=(rows // blk,),
        in_specs=[pl.BlockSpec((blk, 128), lambda i: (i, 0)), _const_spec((1, 128))],
        out_specs=[pl.BlockSpec((blk, 128), lambda i: (i, 0))] * 2,
        out_shape=[jax.ShapeDtypeStruct((rows, 128), F32)] * 2,
        compiler_params=_params("arbitrary"),
        name="rope",
    )(pos_rep, freq_row)
    return jnp.concatenate(
        [cos_d.reshape(t, half), sin_d.reshape(t, half), jnp.zeros((t, 128 - 2 * half), F32)], axis=1)


def _s5par_kernel(lr_ref, li_ref, ldt_ref, bre_ref, bim_ref, ar_ref, ai_ref, bbr_ref, bbi_ref):
    lr = lr_ref[...]
    li = li_ref[...]
    dt = jnp.exp(ldt_ref[...])
    mag = jnp.exp(lr * dt)
    ar = mag * jnp.cos(li * dt)
    ai = mag * jnp.sin(li * dt)
    den = lr * lr + li * li
    nr = ar - 1.0
    ni = ai
    coef_r = (nr * lr + ni * li) / den
    coef_i = (ni * lr - nr * li) / den
    ar_ref[...] = ar
    ai_ref[...] = ai
    bre = bre_ref[...]
    bim = bim_ref[...]
    cr = coef_r[:, None, :]
    ci = coef_i[:, None, :]
    bbr_ref[...] = cr * bre - ci * bim
    bbi_ref[...] = cr * bim + ci * bre


def _s5_params(lam_re, lam_im, log_dt, b_re, b_im):
    g, p = lam_re.shape
    c = b_re.shape[-1]
    bre_t = jnp.transpose(b_re, (0, 2, 1))
    bim_t = jnp.transpose(b_im, (0, 2, 1))
    return pl.pallas_call(
        _s5par_kernel,
        out_shape=[jax.ShapeDtypeStruct((g, p), F32), jax.ShapeDtypeStruct((g, p), F32),
                   jax.ShapeDtypeStruct((g, c, p), F32), jax.ShapeDtypeStruct((g, c, p), F32)],
        name="s5par",
    )(lam_re, lam_im, log_dt.reshape(g, 1), bre_t, bim_t)


def _block_diag(blocks):
    g, r, c = blocks.shape
    eye = jnp.eye(g, dtype=blocks.dtype)
    return (blocks[:, :, None, :] * eye[:, None, :, None]).reshape(g * r, g * c)


@functools.lru_cache(maxsize=None)
def _layout_constants():
    hp, nh = HEAD_PAD, N_HEADS
    width = nh * hp
    half = QK_ROPE // 2
    place = np.zeros((128, width), np.float32)
    hsel = np.zeros((width, 128), np.float32)
    rot = np.zeros((width, width), np.float32)
    ecos = np.zeros((128, width), np.float32)
    esin = np.zeros((128, width), np.float32)
    for h in range(nh):
        base = h * hp
        hsel[base:base + QK_HEAD, h] = 1.0
        for j in range(QK_ROPE):
            place[j, base + QK_NOPE + j] = 1.0
        for j in range(half):
            c1 = base + QK_NOPE + j
            c2 = c1 + half
            rot[c2, c1] = -1.0
            rot[c1, c2] = 1.0
            ecos[j, c1] = 1.0
            ecos[j, c2] = 1.0
            esin[half + j, c1] = 1.0
            esin[half + j, c2] = 1.0
    ones_nope = (ecos.sum(axis=0, keepdims=True) == 0).astype(np.float32)
    return place, hsel, rot, ecos, esin, ones_nope


def _pad_heads(w, head_dim):
    k = w.shape[0]
    w = w.reshape(k, N_HEADS, head_dim)
    w = jnp.pad(w, ((0, 0), (0, 0), (0, HEAD_PAD - head_dim)))
    return w.reshape(k, N_HEADS * HEAD_PAD)


def _inproj_kernel(x_ref, scale_ref, shift_ref, nmix_ref, win_ref, qan_ref, wuq_ref, kvan_ref,
                   wk_ref, wv_ref, place_ref, hsel_ref, hselt_ref, qg_ref, kg_ref, rot_ref,
                   ecos_ref, esin_ref, nope_ref, cs_ref,
                   u_ref, q_ref, k_ref, v_ref):
    x = x_ref[...]
    h = _rms(x) * nmix_ref[...]
    h = h * (1.0 + scale_ref[0]) + shift_ref[0]
    proj = _dot(h.astype(BF16), win_ref[...])
    u_ref[...] = proj[:, :D_SSM]
    o1 = D_SSM + Q_LORA
    o2 = o1 + KV_LORA
    cq = _rms(proj[:, D_SSM:o1]) * qan_ref[...]
    ckv = (_rms(proj[:, o1:o2]) * kvan_ref[...]).astype(BF16)
    kr = proj[:, o2:]
    q = _dot(cq.astype(BF16), wuq_ref[...])
    k = _dot(ckv, wk_ref[...]) + _dot_split(kr, place_ref[...])
    v_ref[...] = _dot(ckv, wv_ref[...]).astype(v_ref.dtype)

    cs = cs_ref[...]
    cos = _dot_split(cs, ecos_ref[...]) + nope_ref[...]
    sin = _dot_split(cs, esin_ref[...])

    def head_norm_rope(z, gain):
        ssq = _dot_split(z * z, hsel_ref[...])
        r = lax.rsqrt(ssq * (1.0 / QK_HEAD) + EPS)
        zn = z * _dot_split(r, hselt_ref[...]) * gain
        return zn * cos + _dot(zn.astype(BF16), rot_ref[...]) * sin

    qr = head_norm_rope(q, qg_ref[...]) * (QK_HEAD ** -0.5)
    q_ref[...] = qr.astype(q_ref.dtype)
    k_ref[...] = head_norm_rope(k, kg_ref[...]).astype(k_ref.dtype)


def _inproj(x2, scale_m, shift_m, norm_mix, w_in, q_a_norm, w_uq, kv_a_norm, w_ukv, q_norm, k_norm,
            cs, bsz, seq, tm):
    t = bsz * seq
    nsb = seq // tm
    place, hsel, rot, ecos, esin, ones_nope = _layout_constants()
    width = N_HEADS * HEAD_PAD
    win = jnp.pad(w_in, ((0, 0), (0, D_MODEL - w_in.shape[1]))).astype(BF16)
    wuq = _pad_heads(w_uq, QK_HEAD).astype(BF16)
    wkv = w_ukv.reshape(KV_LORA, N_HEADS, QK_NOPE + V_HEAD)
    wk = _pad_heads(wkv[:, :, :QK_NOPE].reshape(KV_LORA, N_HEADS * QK_NOPE), QK_NOPE).astype(BF16)
    wv = wkv[:, :, QK_NOPE:].reshape(KV_LORA, N_HEADS * V_HEAD).astype(BF16)
    qg = _pad_heads(jnp.tile(q_norm, N_HEADS).reshape(1, -1), QK_HEAD)
    kg = _pad_heads(jnp.tile(k_norm, N_HEADS).reshape(1, -1), QK_HEAD)
    consts = [
        norm_mix.reshape(1, D_MODEL), win, q_a_norm.reshape(1, Q_LORA), wuq,
        kv_a_norm.reshape(1, KV_LORA), wk, wv,
        jnp.asarray(place, BF16), jnp.asarray(hsel, BF16), jnp.asarray(hsel.T.copy(), BF16), qg, kg,
        jnp.asarray(rot, BF16), jnp.asarray(ecos, BF16), jnp.asarray(esin, BF16),
        jnp.asarray(ones_nope, F32),
    ]
    mod_spec = pl.BlockSpec((1, 1, D_MODEL), lambda i: (i // nsb, 0, 0))
    in_specs = ([pl.BlockSpec((tm, D_MODEL), lambda i: (i, 0)), mod_spec, mod_spec]
                + [_const_spec(a.shape) for a in consts]
                + [pl.BlockSpec((tm, 128), lambda i: (i, 0))])
    out_specs = [
        pl.BlockSpec((tm, D_SSM), lambda i: (i % nsb, i // nsb)),
        pl.BlockSpec((tm, width), lambda i: (i, 0)),
        pl.BlockSpec((tm, width), lambda i: (i, 0)),
        pl.BlockSpec((tm, N_HEADS * V_HEAD), lambda i: (i, 0)),
    ]
    out_shape = [
        jax.ShapeDtypeStruct((seq, bsz * D_SSM), F32),
        jax.ShapeDtypeStruct((t, width), BF16),
        jax.ShapeDtypeStruct((t, width), BF16),
        jax.ShapeDtypeStruct((t, N_HEADS * V_HEAD), BF16),
    ]
    return pl.pallas_call(
        _inproj_kernel, grid=(t // tm,), in_specs=in_specs, out_specs=out_specs, out_shape=out_shape,
        compiler_params=_params("arbitrary"), name="inproj",
    )(x2, scale_m, shift_m, *consts, cs)


def _attn_kernel(q_ref, k_ref, v_ref, o_ref, *, tq):
    qi = pl.program_id(2)
    q0 = q_ref[:, :HEAD_PAD]
    q1 = q_ref[:, HEAD_PAD:]
    lane = lax.broadcasted_iota(jnp.int32, (1, 2 * V_HEAD), 1)
    first = lane < V_HEAD

    def block(j, carry, masked):
        m0, l0, m1, l1, acc = carry
        r0 = pl.multiple_of(j * tq, tq)
        kb = k_ref[pl.ds(r0, tq), :]
        vb = v_ref[pl.ds(r0, tq), :]
        s0 = _dot_nt(q0, kb[:, :HEAD_PAD])
        s1 = _dot_nt(q1, kb[:, HEAD_PAD:])
        if masked:
            row = lax.broadcasted_iota(jnp.int32, (tq, tq), 0)
            col = lax.broadcasted_iota(jnp.int32, (tq, tq), 1)
            keep = col <= row
            s0 = jnp.where(keep, s0, -jnp.inf)
            s1 = jnp.where(keep, s1, -jnp.inf)
        n0 = jnp.maximum(m0, jnp.max(s0, axis=-1, keepdims=True))
        n1 = jnp.maximum(m1, jnp.max(s1, axis=-1, keepdims=True))
        a0 = jnp.exp(m0 - n0)
        a1 = jnp.exp(m1 - n1)
        p0 = jnp.exp(s0 - n0)
        p1 = jnp.exp(s1 - n1)
        l0 = a0 * l0 + jnp.sum(p0, axis=-1, keepdims=True)
        l1 = a1 * l1 + jnp.sum(p1, axis=-1, keepdims=True)
        zero = jnp.zeros_like(vb)
        pv = (_dot(p0.astype(BF16), jnp.where(first, vb, zero))
              + _dot(p1.astype(BF16), jnp.where(first, zero, vb)))
        acc = acc * jnp.where(first, a0, a1) + pv
        return n0, l0, n1, l1, acc

    neg = jnp.full((tq, 1), -jnp.inf, F32)
    zl = jnp.zeros((tq, 1), F32)
    carry = (neg, zl, neg, zl, jnp.zeros((tq, 2 * V_HEAD), F32))
    carry = lax.fori_loop(0, qi, lambda j, c: block(j, c, False), carry)
    _, l0, _, l1, acc = block(qi, carry, True)
    o_ref[...] = (acc * jnp.where(first, 1.0 / l0, 1.0 / l1)).astype(o_ref.dtype)


def _attention(q, k, v, bsz, seq, tq):
    nq = seq // tq
    t = bsz * seq
    return pl.pallas_call(
        functools.partial(_attn_kernel, tq=tq),
        grid=(bsz, N_HEADS // 2, nq),
        in_specs=[pl.BlockSpec((tq, 2 * HEAD_PAD), lambda b, h, i: (b * nq + i, h)),
                  pl.BlockSpec((seq, 2 * HEAD_PAD), lambda b, h, i: (b, h)),
                  pl.BlockSpec((seq, 2 * V_HEAD), lambda b, h, i: (b, h))],
        out_specs=pl.BlockSpec((tq, 2 * V_HEAD), lambda b, h, i: (b * nq + i, h)),
        out_shape=jax.ShapeDtypeStruct((t, N_HEADS * V_HEAD), BF16),
        compiler_params=_params("arbitrary", "arbitrary", "arbitrary"),
        name="attn",
    )(q, k, v)


def _s5_kernel(u_ref, bd_ref, ar_ref, ai_ref, cd_ref, dsk_ref, wglu_ref, bglu_ref, gain_ref,
               y_ref, bu_ref, st_ref, *, lc, bsz, cb):
    @pl.when(pl.program_id(0) == 0)
    def _():
        st_ref[...] = jnp.zeros_like(st_ref)

    u = u_ref[...]
    bu_ref[...] = _dot(u.astype(BF16), bd_ref[...])

    for c0 in range(0, N_STATE, cb):
        ar = ar_ref[:, c0:c0 + cb]
        ai = ai_ref[:, c0:c0 + cb]

        def step(t, carry, c0=c0, ar=ar, ai=ai):
            xr, xi = carry
            r0 = pl.multiple_of(t * bsz, bsz)
            bur = bu_ref[pl.ds(r0, bsz), c0:c0 + cb]
            bui = bu_ref[pl.ds(r0, bsz), N_STATE + c0:N_STATE + c0 + cb]
            nxr = ar * xr - ai * xi + bur
            nxi = ar * xi + ai * xr + bui
            bu_ref[pl.ds(r0, bsz), c0:c0 + cb] = nxr
            bu_ref[pl.ds(r0, bsz), N_STATE + c0:N_STATE + c0 + cb] = nxi
            return nxr, nxi

        xr0 = st_ref[:, c0:c0 + cb]
        xi0 = st_ref[:, N_STATE + c0:N_STATE + c0 + cb]
        xr, xi = lax.fori_loop(0, lc, step, (xr0, xi0))
        st_ref[:, c0:c0 + cb] = xr
        st_ref[:, N_STATE + c0:N_STATE + c0 + cb] = xi

    y = _dot(bu_ref[...].astype(BF16), cd_ref[...]) + dsk_ref[...] * u
    y = _gelu(y)
    y = y * jax.nn.sigmoid(_dot(y.astype(BF16), wglu_ref[...]) + bglu_ref[...])
    y_ref[...] = (_rms(y) * gain_ref[...]).astype(y_ref.dtype)


def _s5(u_tm, bd, ar, ai, cd, d_skip, w_glu, b_glu, gain, bsz, seq, lc):
    rows = lc * bsz
    consts = [bd, ar, ai, cd, d_skip.reshape(1, D_SSM), w_glu.astype(BF16), b_glu.reshape(1, D_SSM),
              gain.reshape(1, D_SSM)]
    return pl.pallas_call(
        functools.partial(_s5_kernel, lc=lc, bsz=bsz, cb=512),
        grid=(seq // lc,),
        in_specs=[pl.BlockSpec((rows, D_SSM), lambda j: (j, 0))] + [_const_spec(a.shape) for a in consts],
        out_specs=pl.BlockSpec((rows, D_SSM), lambda j: (j, 0)),
        out_shape=jax.ShapeDtypeStruct((seq * bsz, D_SSM), BF16),
        scratch_shapes=[pltpu.VMEM((rows, 2 * N_STATE), F32), pltpu.VMEM((bsz, 2 * N_STATE), F32)],
        compiler_params=_params("arbitrary"),
        name="s5",
    )(u_tm, *consts)


def _outproj_kernel(ys_ref, ya_ref, x_ref, gate_ref, scale_ref, shift_ref, ga_ref, wos_ref, woa_ref,
                    nffn_ref, wq_ref, x1_ref, h2_ref, qp_ref):
    ya = _rms(ya_ref[...].astype(F32)) * ga_ref[...]
    y = _dot(ys_ref[...], wos_ref[...]) + _dot(ya.astype(BF16), woa_ref[...])
    x1 = x_ref[...] + gate_ref[0] * y
    x1_ref[...] = x1
    h2 = _rms(x1) * nffn_ref[...]
    h2 = h2 * (1.0 + scale_ref[0]) + shift_ref[0]
    hi = h2.astype(BF16)
    lo = (h2 - hi.astype(F32)).astype(BF16)
    h2_ref[...] = jnp.concatenate(
        [part[:, c * 128:(c + 1) * 128] for part in (hi, lo) for c in _CHUNK_OF_PACKED], axis=1)
    qp_ref[...] = _dot(hi, wq_ref[...]).astype(qp_ref.dtype)


def _outproj(ys_tm, ya, x2, gate_m, scale_f, shift_f, out_norm_attn, w_out, norm_ffn, w_query,
             bsz, seq, tm):
    t = bsz * seq
    nsb = seq // tm
    nq = w_query.shape[1]
    wo = w_out.astype(BF16)
    consts = [out_norm_attn.reshape(1, -1), wo[:D_SSM], wo[D_SSM:], norm_ffn.reshape(1, D_MODEL),
              w_query.astype(BF16)]
    mod_spec = pl.BlockSpec((1, 1, D_MODEL), lambda i: (i // nsb, 0, 0))
    row_spec = pl.BlockSpec((tm, D_MODEL), lambda i: (i, 0))
    return pl.pallas_call(
        _outproj_kernel,
        grid=(t // tm,),
        in_specs=[pl.BlockSpec((tm, D_SSM), lambda i: (i % nsb, i // nsb)),
                  pl.BlockSpec((tm, D_SSM), lambda i: (i, 0)),
                  row_spec, mod_spec, mod_spec, mod_spec] + [_const_spec(a.shape) for a in consts],
        out_specs=[row_spec, pl.BlockSpec((tm, 2 * D_MODEL), lambda i: (i, 0)),
                   pl.BlockSpec((tm, nq), lambda i: (i, 0))],
        out_shape=[jax.ShapeDtypeStruct((t, D_MODEL), F32), jax.ShapeDtypeStruct((t, 2 * D_MODEL), BF16),
                   jax.ShapeDtypeStruct((t, nq), BF16)],
        compiler_params=_params("arbitrary"),
        name="outproj",
    )(ys_tm, ya, x2, gate_m, scale_f, shift_f, *consts)


def _topk_rows(s, k):
    n_rows = s.shape[0]
    iota = lax.broadcasted_iota(jnp.int32, s.shape, 0)
    vals, idxs = [], []
    for _ in range(k):
        m = jnp.max(s, axis=0, keepdims=True)
        ix = jnp.min(jnp.where(s == m, iota, n_rows), axis=0, keepdims=True)
        vals.append(m)
        idxs.append(ix)
        s = jnp.where(iota == ix, -jnp.inf, s)
    return jnp.concatenate(vals, axis=0), jnp.concatenate(idxs, axis=0)


def _pick_rows(sel, table):
    out = jnp.zeros(sel.shape, table.dtype)
    for a in range(table.shape[0]):
        out = jnp.where(sel == a, table[a:a + 1, :], out)
    return out


def _topk_kernel(q_ref, keys_ref, idx_ref, g_ref):
    k = PEER_TOPK
    s1 = _dot_nt(keys_ref[0], q_ref[:, :PEER_HALF])
    s2 = _dot_nt(keys_ref[1], q_ref[:, PEER_HALF:])
    v1, i1 = _topk_rows(s1, k)
    v2, i2 = _topk_rows(s2, k)
    cand = jnp.concatenate([v1[a:a + 1, :] + v2 for a in range(k)], axis=0)
    top_s, top_c = _topk_rows(cand, k)
    e1 = _pick_rows(top_c >> 4, i1)
    e2 = _pick_rows(top_c & (k - 1), i2)
    idx_ref[0] = (e1 * PEER_KEYS + e2) * _WORD_ROWS
    e = jnp.exp(top_s - top_s[0:1, :])
    g_ref[0] = e / jnp.sum(e, axis=0, keepdims=True)


def _topk(qp, sub_keys, tm):
    t = qp.shape[0]
    return pl.pallas_call(
        _topk_kernel,
        grid=(t // tm, PEER_HEADS),
        in_specs=[pl.BlockSpec((tm, 2 * PEER_HALF), lambda i, h: (i, h)),
                  _const_spec(sub_keys.shape)],
        out_specs=[pl.BlockSpec((1, PEER_TOPK, tm), lambda i, h: (h, 0, i))] * 2,
        out_shape=[jax.ShapeDtypeStruct((PEER_HEADS, PEER_TOPK, t), jnp.int32),
                   jax.ShapeDtypeStruct((PEER_HEADS, PEER_TOPK, t), F32)],
        compiler_params=_params("arbitrary", "arbitrary"),
        name="topk",
    )(qp, sub_keys.astype(BF16))


_CHUNK_OF_PACKED = tuple((q % 2) * 4 + q // 2 for q in range(8))
_WORD_ROWS = D_MODEL // 256


def _pack_table(w):
    e, d = w.shape
    bits = lax.bitcast_convert_type(w.astype(BF16), jnp.uint16).astype(jnp.uint32)
    words = bits[:, :d // 2] | (bits[:, d // 2:] << 16)
    return lax.bitcast_convert_type(words, jnp.int32).reshape(e * _WORD_ROWS, 128)


@functools.lru_cache(maxsize=None)
def _peer_constants():
    lane_q = np.arange(8 * N_SEL) % 8
    expand = np.zeros((N_SEL, 8 * N_SEL), np.float32)
    expand[np.arange(8 * N_SEL) // 8, np.arange(8 * N_SEL)] = 1.0
    msk_dn = (lane_q[None, :] == np.arange(8)[:, None]).astype(np.float32)
    packed_of_chunk = np.argsort(np.array(_CHUNK_OF_PACKED))
    msk_up = (lane_q[None, :] == packed_of_chunk[:, None]).astype(np.float32)
    return expand, msk_dn, msk_up


def _gather_rows(idx_ref, t, tbl_ref, g_ref):
    for k in range(N_SEL):
        row = pl.multiple_of(idx_ref[t, k], _WORD_ROWS)
        g_ref[_WORD_ROWS * k:_WORD_ROWS * (k + 1), :] = tbl_ref[pl.ds(row, _WORD_ROWS), :]


def _pipelined_tokens(tt, stage, compute, finish):
    stage(0, 0)
    stage(1, 1)

    def pair(i, _):
        t0 = 2 * i
        r0 = compute(t0, 0)
        r1 = compute(t0 + 1, 1)
        stage(jnp.minimum(t0 + 2, tt - 1), 0)
        stage(jnp.minimum(t0 + 3, tt - 1), 1)
        finish(t0, r0)
        finish(t0 + 1, r1)
        return 0

    lax.fori_loop(0, tt // 2, pair, 0)


def _peer_dn_kernel(idx_ref, hm_ref, tbl_ref, msk_ref, et_ref, a_ref, g0_ref, g1_ref, y_ref, *, tt):
    bufs = (g0_ref, g1_ref)

    def stage(t, slot):
        _gather_rows(idx_ref, t, tbl_ref, bufs[slot])

    def compute(t, slot):
        rows = pltpu.bitcast(bufs[slot][...], BF16)
        return _dot_nt(hm_ref[t], rows)

    def finish(t, y):
        y_ref[t] = y[:8] + y[8:]

    _pipelined_tokens(tt, stage, compute, finish)
    s = jnp.sum(y_ref[...] * msk_ref[...], axis=1)
    a_ref[...] = _dot_split(s, et_ref[...])


def _peer_dn(idx, hm, tbl, tt):
    t = hm.shape[0]
    expand, msk_dn, _ = _peer_constants()
    consts = [jnp.asarray(msk_dn, F32), jnp.asarray(expand.T.copy(), BF16)]
    return pl.pallas_call(
        functools.partial(_peer_dn_kernel, tt=tt),
        grid=(t // tt,),
        in_specs=[pl.BlockSpec((tt, N_SEL), lambda i: (i, 0), memory_space=pltpu.SMEM),
                  pl.BlockSpec((tt, 16, 128), lambda i: (i, 0, 0)),
                  pl.BlockSpec(memory_space=pltpu.VMEM)] + [_const_spec(a.shape) for a in consts],
        out_specs=pl.BlockSpec((tt, N_SEL), lambda i: (i, 0)),
        out_shape=jax.ShapeDtypeStruct((t, N_SEL), F32),
        scratch_shapes=[pltpu.VMEM((4 * N_SEL, 128), jnp.int32), pltpu.VMEM((4 * N_SEL, 128), jnp.int32),
                        pltpu.VMEM((tt, 8, 8 * N_SEL), F32)],
        compiler_params=_params("arbitrary"),
        name="peer_dn",
    )(idx, hm, tbl, *consts)


def _gate_kernel(a_ref, g_ref, w_ref):
    w_ref[...] = g_ref[...] * _gelu(a_ref[...])


def _gate(a, g):
    t = a.shape[0]
    blk = min(t, 2048)
    spec = pl.BlockSpec((blk, N_SEL), lambda i: (i, 0))
    return pl.pallas_call(
        _gate_kernel, grid=(t // blk,), in_specs=[spec, spec], out_specs=spec,
        out_shape=jax.ShapeDtypeStruct((t, N_SEL), F32),
        compiler_params=_params("arbitrary"), name="gate",
    )(a, g)


def _peer_up_kernel(idx_ref, w_ref, x1_ref, gate_ref, tbl_ref, e_ref, msk_ref, o_ref,
                    g0_ref, g1_ref, wexp_ref, lhs_ref, *, tt):
    wexp_ref[...] = _dot_split(w_ref[...], e_ref[...])
    gate = gate_ref[0]
    msk = msk_ref[...]
    top = lax.broadcasted_iota(jnp.int32, msk.shape, 0) < 8
    bufs = (g0_ref, g1_ref)

    def stage(t, slot):
        _gather_rows(idx_ref, t, tbl_ref, bufs[slot])
        wsel = wexp_ref[pl.ds(t, 1), :] * msk
        hi = wsel.astype(BF16)
        lo = (wsel - hi.astype(F32)).astype(BF16)
        lhs_ref[slot] = jnp.where(top, hi, lo)

    def compute(t, slot):
        rows = pltpu.bitcast(bufs[slot][...], BF16)
        return _dot(lhs_ref[slot], rows)

    def finish(t, o16):
        o_ref[t] = x1_ref[t] + gate * (o16[:8] + o16[8:])

    _pipelined_tokens(tt, stage, compute, finish)


def _peer_up(idx, wgt, x13, gate3, tbl, seq, tt):
    t = x13.shape[0]
    nsb = seq // tt
    expand, _, msk_up = _peer_constants()
    consts = [jnp.asarray(expand, BF16), jnp.asarray(np.concatenate([msk_up, msk_up], axis=0), F32)]
    row = pl.BlockSpec((tt, 8, 128), lambda i: (i, 0, 0))
    return pl.pallas_call(
        functools.partial(_peer_up_kernel, tt=tt),
        grid=(t // tt,),
        in_specs=[pl.BlockSpec((tt, N_SEL), lambda i: (i, 0), memory_space=pltpu.SMEM),
                  pl.BlockSpec((tt, N_SEL), lambda i: (i, 0)),
                  row, pl.BlockSpec((1, 8, 128), lambda i: (i // nsb, 0, 0)),
                  pl.BlockSpec(memory_space=pltpu.VMEM)] + [_const_spec(a.shape) for a in consts],
        out_specs=row,
        out_shape=jax.ShapeDtypeStruct((t, 8, 128), F32),
        scratch_shapes=[pltpu.VMEM((4 * N_SEL, 128), jnp.int32), pltpu.VMEM((4 * N_SEL, 128), jnp.int32),
                        pltpu.VMEM((tt, 8 * N_SEL), F32), pltpu.VMEM((2, 16, 8 * N_SEL), BF16)],
        compiler_params=_params("arbitrary"),
        name="peer_up",
    )(idx, wgt, x13, gate3, tbl, *consts)


def _layer(x, c, positions, w_ada, b_ada, norm_mix, norm_ffn, w_in, lam_re, lam_im, log_dt, b_re, b_im,
           c_re, c_im, d_skip, w_glu, b_glu, q_a_norm, w_uq, kv_a_norm, w_ukv, q_norm, k_norm,
           out_norm_ssm, out_norm_attn, w_out, w_query, sub_keys, expert_down, expert_up):
    bsz, seq, _ = x.shape
    t = bsz * seq
    tm = min(256, seq)
    x2 = x.reshape(t, D_MODEL)

    mod = _ada(c, w_ada, b_ada)
    shift_m, scale_m, gate_m, shift_f, scale_f, gate_f = [
        m.reshape(bsz, 1, D_MODEL) for m in jnp.split(mod, N_ADA, axis=-1)]

    cs = _rope_tables(positions)
    u_tm, q, k, v = _inproj(x2, scale_m, shift_m, norm_mix, w_in, q_a_norm, w_uq, kv_a_norm, w_ukv,
                            q_norm, k_norm, cs, bsz, seq, tm)
    ya = _attention(q, k, v, bsz, seq, tm)

    ar, ai, bbr, bbi = _s5_params(lam_re, lam_im, log_dt, b_re, b_im)
    bd = jnp.concatenate([_block_diag(bbr), _block_diag(bbi)], axis=1).astype(BF16)
    cd = jnp.concatenate([_block_diag(jnp.transpose(c_re, (0, 2, 1))),
                          _block_diag(jnp.transpose(-c_im, (0, 2, 1)))], axis=0).astype(BF16)
    ys_tm = _s5(u_tm.reshape(seq * bsz, D_SSM), bd, ar.reshape(1, N_STATE), ai.reshape(1, N_STATE), cd,
                d_skip, w_glu, b_glu, out_norm_ssm, bsz, seq, min(32, seq))

    x1, hm, qp = _outproj(ys_tm.reshape(seq, bsz * D_SSM), ya, x2, gate_m, scale_f, shift_f,
                          out_norm_attn, w_out, norm_ffn, w_query, bsz, seq, tm)

    idx_t, g_t = _topk(qp, sub_keys, tm)
    idx = jnp.transpose(idx_t.reshape(N_SEL, t))
    g = jnp.transpose(g_t.reshape(N_SEL, t))

    tt = min(128, seq)
    a = _peer_dn(idx, hm.reshape(t, 16, 128), _pack_table(expert_down), tt)
    wgt = _gate(a, g)
    out = _peer_up(idx, wgt, x1.reshape(t, 8, 128), gate_f.reshape(bsz, 8, 128),
                   _pack_table(expert_up), seq, tt)
    return out.reshape(bsz, seq, D_MODEL)


def kernel(x, c, positions, w_ada, b_ada, norm_mix, norm_ffn, w_in, lam_re, lam_im, log_dt, b_re, b_im, c_re, c_im, d_skip, w_glu, b_glu, q_a_norm, w_uq, kv_a_norm, w_ukv, q_norm, k_norm, out_norm_ssm, out_norm_attn, w_out, w_query, sub_keys, expert_down, expert_up):
    for l in range(w_ada.shape[0]):
        x = _layer(x, c, positions, w_ada[l], b_ada[l], norm_mix[l], norm_ffn[l], w_in[l],
                   lam_re[l], lam_im[l], log_dt[l], b_re[l], b_im[l], c_re[l], c_im[l],
                   d_skip[l], w_glu[l], b_glu[l], q_a_norm[l], w_uq[l], kv_a_norm[l], w_ukv[l],
                   q_norm[l], k_norm[l], out_norm_ssm[l], out_norm_attn[l], w_out[l],
                   w_query[l], sub_keys[l], expert_down[l], expert_up[l])
    return x
```

```python
import functools
import math

import numpy as np
import jax
import jax.numpy as jnp
from jax import lax
from jax.experimental import pallas as pl
from jax.experimental.pallas import tpu as pltpu

F32 = jnp.float32
BF16 = jnp.bfloat16

D_MODEL = 1024
D_SSM = 512
SSM_GROUP = 16
N_SSM_GROUPS = 32
SSM_STATE = 64
N_STATE = N_SSM_GROUPS * SSM_STATE
N_HEADS = 8
QK_NOPE = 64
QK_ROPE = 32
QK_HEAD = 96
V_HEAD = 64
HEAD_PAD = 128
Q_LORA = 256
KV_LORA = 128
ROPE_THETA = 10000.0
PEER_HEADS = 8
PEER_KEYS = 128
PEER_TOPK = 16
PEER_HALF = 128
N_SEL = PEER_HEADS * PEER_TOPK
N_ADA = 6
EPS = 1e-6
GELU_C = math.sqrt(2.0 / math.pi)

VMEM_LIMIT = 48 * 1024 * 1024


def _dot(a, b):
    return jnp.dot(a, b, preferred_element_type=F32)


def _dot_nt(a, b):
    return lax.dot_general(a, b, (((1,), (1,)), ((), ())), preferred_element_type=F32)


def _dot_split(a, sel):
    hi = a.astype(BF16)
    lo = (a - hi.astype(F32)).astype(BF16)
    return _dot(hi, sel) + _dot(lo, sel)


def _rms(x):
    return x * lax.rsqrt(jnp.mean(x * x, axis=-1, keepdims=True) + EPS)


def _gelu(x):
    return 0.5 * x * (1.0 + jnp.tanh(GELU_C * (x + 0.044715 * x * x * x)))


def _params(*sem):
    return pltpu.CompilerParams(dimension_semantics=sem, vmem_limit_bytes=VMEM_LIMIT)


def _const_spec(shape):
    nd = len(shape)
    return pl.BlockSpec(shape, lambda *_: (0,) * nd)


def _ada_kernel(c_ref, w_ref, b_ref, o_ref):
    c = c_ref[...]
    s = c * jax.nn.sigmoid(c)
    o_ref[...] = _dot_split2(s, w_ref[...]) + b_ref[...]


def _dot_split2(a, w):
    ah = a.astype(BF16)
    al = (a - ah.astype(F32)).astype(BF16)
    wh = w.astype(BF16)
    wl = (w - wh.astype(F32)).astype(BF16)
    return _dot(ah, wh) + (_dot(ah, wl) + _dot(al, wh))


def _ada(c, w_ada, b_ada):
    bsz = c.shape[0]
    n = w_ada.shape[1]
    blk = D_MODEL
    return pl.pallas_call(
        _ada_kernel,
        grid=(n // blk,),
        in_specs=[_const_spec((bsz, D_MODEL)),
                  pl.BlockSpec((D_MODEL, blk), lambda j: (0, j)),
                  pl.BlockSpec((1, blk), lambda j: (0, j))],
        out_specs=pl.BlockSpec((bsz, blk), lambda j: (0, j)),
        out_shape=jax.ShapeDtypeStruct((bsz, n), F32),
        compiler_params=_params("arbitrary"),
        name="ada",
    )(c, w_ada, b_ada.reshape(1, n))


def _rope_kernel(pos_ref, freq_ref, cos_ref, sin_ref):
    ang = pos_ref[...].astype(F32) * freq_ref[...]
    cos_ref[...] = jnp.cos(ang)
    sin_ref[...] = jnp.sin(ang)


def _rope_tables(positions):
    half = QK_ROPE // 2
    t = positions.size
    rows = t * half // 128
    pos_rep = jnp.repeat(positions.reshape(-1), half).reshape(rows, 128)
    inv_freq = ROPE_THETA ** (-jnp.arange(half, dtype=F32) / half)
    freq_row = jnp.tile(inv_freq, 128 // half).reshape(1, 128)
    blk = min(rows, 512)
    cos_d, sin_d = pl.pallas_call(
        _rope_kernel,
        grid=(rows // blk,),
        in_specs=[pl.BlockSpec((blk, 128), lambda i: (i, 0)), _const_spec((1, 128))],
        out_specs=[pl.BlockSpec((blk, 128), lambda i: (i, 0))] * 2,
        out_shape=[jax.ShapeDtypeStruct((rows, 128), F32)] * 2,
        compiler_params=_params("arbitrary"),
        name="rope",
    )(pos_rep, freq_row)
    return jnp.concatenate(
        [cos_d.reshape(t, half), sin_d.reshape(t, half), jnp.zeros((t, 128 - 2 * half), F32)], axis=1)


def _s5par_kernel(lr_ref, li_ref, ldt_ref, bre_ref, bim_ref, ar_ref, ai_ref, bbr_ref, bbi_ref):
    lr = lr_ref[...]
    li = li_ref[...]
    dt = jnp.exp(ldt_ref[...])
    mag = jnp.exp(lr * dt)
    ar = mag * jnp.cos(li * dt)
    ai = mag * jnp.sin(li * dt)
    den = lr * lr + li * li
    nr = ar - 1.0
    ni = ai
    coef_r = (nr * lr + ni * li) / den
    coef_i = (ni * lr - nr * li) / den
    ar_ref[...] = ar
    ai_ref[...] = ai
    bre = bre_ref[...]
    bim = bim_ref[...]
    cr = coef_r[:, None, :]
    ci = coef_i[:, None, :]
    bbr_ref[...] = cr * bre - ci * bim
    bbi_ref[...] = cr * bim + ci * bre


def _s5_params(lam_re, lam_im, log_dt, b_re, b_im):
    g, p = lam_re.shape
    c = b_re.shape[-1]
    bre_t = jnp.transpose(b_re, (0, 2, 1))
    bim_t = jnp.transpose(b_im, (0, 2, 1))
    return pl.pallas_call(
        _s5par_kernel,
        out_shape=[jax.ShapeDtypeStruct((g, p), F32), jax.ShapeDtypeStruct((g, p), F32),
                   jax.ShapeDtypeStruct((g, c, p), F32), jax.ShapeDtypeStruct((g, c, p), F32)],
        name="s5par",
    )(lam_re, lam_im, log_dt.reshape(g, 1), bre_t, bim_t)


def _block_diag(blocks):
    g, r, c = blocks.shape
    eye = jnp.eye(g, dtype=blocks.dtype)
    return (blocks[:, :, None, :] * eye[:, None, :, None]).reshape(g * r, g * c)


@functools.lru_cache(maxsize=None)
def _layout_constants():
    hp, nh = HEAD_PAD, N_HEADS
    width = nh * hp
    half = QK_ROPE // 2
    place = np.zeros((128, width), np.float32)
    hsel = np.zeros((width, 128), np.float32)
    rot = np.zeros((width, width), np.float32)
    ecos = np.zeros((128, width), np.float32)
    esin = np.zeros((128, width), np.float32)
    for h in range(nh):
        base = h * hp
        hsel[base:base + QK_HEAD, h] = 1.0
        for j in range(QK_ROPE):
            place[j, base + QK_NOPE + j] = 1.0
        for j in range(half):
            c1 = base + QK_NOPE + j
            c2 = c1 + half
            rot[c2, c1] = -1.0
            rot[c1, c2] = 1.0
            ecos[j, c1] = 1.0
            ecos[j, c2] = 1.0
            esin[half + j, c1] = 1.0
            esin[half + j, c2] = 1.0
    ones_nope = (ecos.sum(axis=0, keepdims=True) == 0).astype(np.float32)
    return place, hsel, rot, ecos, esin, ones_nope


def _pad_heads(w, head_dim):
    k = w.shape[0]
    w = w.reshape(k, N_HEADS, head_dim)
    w = jnp.pad(w, ((0, 0), (0, 0), (0, HEAD_PAD - head_dim)))
    return w.reshape(k, N_HEADS * HEAD_PAD)


def _inproj_kernel(x_ref, scale_ref, shift_ref, nmix_ref, win_ref, qan_ref, wuq_ref, kvan_ref,
                   wk_ref, wv_ref, place_ref, hsel_ref, hselt_ref, qg_ref, kg_ref, rot_ref,
                   ecos_ref, esin_ref, nope_ref, cs_ref,
                   u_ref, q_ref, k_ref, v_ref):
    x = x_ref[...]
    h = _rms(x) * nmix_ref[...]
    h = h * (1.0 + scale_ref[0]) + shift_ref[0]
    proj = _dot(h.astype(BF16), win_ref[...])
    u_ref[...] = proj[:, :D_SSM]
    o1 = D_SSM + Q_LORA
    o2 = o1 + KV_LORA
    cq = _rms(proj[:, D_SSM:o1]) * qan_ref[...]
    ckv = (_rms(proj[:, o1:o2]) * kvan_ref[...]).astype(BF16)
    kr = proj[:, o2:]
    q = _dot(cq.astype(BF16), wuq_ref[...])
    k = _dot(ckv, wk_ref[...]) + _dot_split(kr, place_ref[...])
    v_ref[...] = _dot(ckv, wv_ref[...]).astype(v_ref.dtype)

    cs = cs_ref[...]
    cos = _dot_split(cs, ecos_ref[...]) + nope_ref[...]
    sin = _dot_split(cs, esin_ref[...])

    def head_norm_rope(z, gain):
        ssq = _dot_split(z * z, hsel_ref[...])
        r = lax.rsqrt(ssq * (1.0 / QK_HEAD) + EPS)
        zn = z * _dot_split(r, hselt_ref[...]) * gain
        return zn * cos + _dot(zn.astype(BF16), rot_ref[...]) * sin

    qr = head_norm_rope(q, qg_ref[...]) * (QK_HEAD ** -0.5)
    q_ref[...] = qr.astype(q_ref.dtype)
    k_ref[...] = head_norm_rope(k, kg_ref[...]).astype(k_ref.dtype)


def _inproj(x2, scale_m, shift_m, norm_mix, w_in, q_a_norm, w_uq, kv_a_norm, w_ukv, q_norm, k_norm,
            cs, bsz, seq, tm):
    t = bsz * seq
    nsb = seq // tm
    place, hsel, rot, ecos, esin, ones_nope = _layout_constants()
    width = N_HEADS * HEAD_PAD
    win = jnp.pad(w_in, ((0, 0), (0, D_MODEL - w_in.shape[1]))).astype(BF16)
    wuq = _pad_heads(w_uq, QK_HEAD).astype(BF16)
    wkv = w_ukv.reshape(KV_LORA, N_HEADS, QK_NOPE + V_HEAD)
    wk = _pad_heads(wkv[:, :, :QK_NOPE].reshape(KV_LORA, N_HEADS * QK_NOPE), QK_NOPE).astype(BF16)
    wv = wkv[:, :, QK_NOPE:].reshape(KV_LORA, N_HEADS * V_HEAD).astype(BF16)
    qg = _pad_heads(jnp.tile(q_norm, N_HEADS).reshape(1, -1), QK_HEAD)
    kg = _pad_heads(jnp.tile(k_norm, N_HEADS).reshape(1, -1), QK_HEAD)
    consts = [
        norm_mix.reshape(1, D_MODEL), win, q_a_norm.reshape(1, Q_LORA), wuq,
        kv_a_norm.reshape(1, KV_LORA), wk, wv,
        jnp.asarray(place, BF16), jnp.asarray(hsel, BF16), jnp.asarray(hsel.T.copy(), BF16), qg, kg,
        jnp.asarray(rot, BF16), jnp.asarray(ecos, BF16), jnp.asarray(esin, BF16),
        jnp.asarray(ones_nope, F32),
    ]
    mod_spec = pl.BlockSpec((1, 1, D_MODEL), lambda i: (i // nsb, 0, 0))
    in_specs = ([pl.BlockSpec((tm, D_MODEL), lambda i: (i, 0)), mod_spec, mod_spec]
                + [_const_spec(a.shape) for a in consts]
                + [pl.BlockSpec((tm, 128), lambda i: (i, 0))])
    out_specs = [
        pl.BlockSpec((tm, D_SSM), lambda i: (i % nsb, i // nsb)),
        pl.BlockSpec((tm, width), lambda i: (i, 0)),
        pl.BlockSpec((tm, width), lambda i: (i, 0)),
        pl.BlockSpec((tm, N_HEADS * V_HEAD), lambda i: (i, 0)),
    ]
    out_shape = [
        jax.ShapeDtypeStruct((seq, bsz * D_SSM), F32),
        jax.ShapeDtypeStruct((t, width), BF16),
        jax.ShapeDtypeStruct((t, width), BF16),
        jax.ShapeDtypeStruct((t, N_HEADS * V_HEAD), BF16),
    ]
    return pl.pallas_call(
        _inproj_kernel, grid=(t // tm,), in_specs=in_specs, out_specs=out_specs, out_shape=out_shape,
        compiler_params=_params("arbitrary"), name="inproj",
    )(x2, scale_m, shift_m, *consts, cs)


def _attn_kernel(q_ref, k_ref, v_ref, o_ref, *, tq, rows, pairs):
    qi = pl.program_id(2)
    lane = lax.broadcasted_iota(jnp.int32, (1, 2 * V_HEAD), 1)
    first = lane < V_HEAD
    groups = tq // rows

    def block(j, carry, masked):
        r0 = pl.multiple_of(j * tq, tq)
        out = []
        for hp in range(pairs):
            kb = k_ref[pl.ds(r0, tq), hp * 2 * HEAD_PAD:(hp + 1) * 2 * HEAD_PAD]
            vb = v_ref[pl.ds(r0, tq), hp * 2 * V_HEAD:(hp + 1) * 2 * V_HEAD]
            one = jnp.ones_like(vb)
            v0 = jnp.where(first, vb, one)
            v1 = jnp.where(first, one, vb)
            for rg in range(groups):
                m0, m1, acc0, acc1 = carry[hp * groups + rg]
                q0 = q_ref[rg * rows:(rg + 1) * rows, hp * 2 * HEAD_PAD:hp * 2 * HEAD_PAD + HEAD_PAD]
                q1 = q_ref[rg * rows:(rg + 1) * rows, hp * 2 * HEAD_PAD + HEAD_PAD:(hp + 1) * 2 * HEAD_PAD]
                s0 = _dot_nt(q0, kb[:, :HEAD_PAD])
                s1 = _dot_nt(q1, kb[:, HEAD_PAD:])
                if masked:
                    row = lax.broadcasted_iota(jnp.int32, (rows, tq), 0) + rg * rows
                    col = lax.broadcasted_iota(jnp.int32, (rows, tq), 1)
                    keep = col <= row
                    s0 = jnp.where(keep, s0, -jnp.inf)
                    s1 = jnp.where(keep, s1, -jnp.inf)
                n0 = jnp.maximum(m0, jnp.max(s0, axis=-1, keepdims=True))
                n1 = jnp.maximum(m1, jnp.max(s1, axis=-1, keepdims=True))
                p0 = jnp.exp(s0 - n0).astype(BF16)
                p1 = jnp.exp(s1 - n1).astype(BF16)
                acc0 = jnp.exp(m0 - n0) * acc0 + _dot(p0, v0)
                acc1 = jnp.exp(m1 - n1) * acc1 + _dot(p1, v1)
                out.append((n0, n1, acc0, acc1))
        return tuple(out)

    neg = jnp.full((rows, 1), -jnp.inf, F32)
    zacc = jnp.zeros((rows, 2 * V_HEAD), F32)
    carry = lax.fori_loop(0, qi, lambda j, c: block(j, c, False), ((neg, neg, zacc, zacc),) * (pairs * groups))
    final = block(qi, carry, True)
    for hp in range(pairs):
        for rg in range(groups):
            _, _, acc0, acc1 = final[hp * groups + rg]
            out0 = acc0 * pltpu.roll(1.0 / acc0, V_HEAD, axis=1)
            out1 = acc1 * pltpu.roll(1.0 / acc1, V_HEAD, axis=1)
            o_ref[rg * rows:(rg + 1) * rows, hp * 2 * V_HEAD:(hp + 1) * 2 * V_HEAD] = (
                jnp.where(first, out0, out1).astype(o_ref.dtype))


def _attention(q, k, v, bsz, seq, tq, pairs=1):
    nq = seq // tq
    t = bsz * seq
    return pl.pallas_call(
        functools.partial(_attn_kernel, tq=tq, rows=tq, pairs=pairs),
        grid=(bsz, N_HEADS // (2 * pairs), nq),
        in_specs=[pl.BlockSpec((tq, pairs * 2 * HEAD_PAD), lambda b, h, i: (b * nq + i, h)),
                  pl.BlockSpec((seq, pairs * 2 * HEAD_PAD), lambda b, h, i: (b, h)),
                  pl.BlockSpec((seq, pairs * 2 * V_HEAD), lambda b, h, i: (b, h))],
        out_specs=pl.BlockSpec((tq, pairs * 2 * V_HEAD), lambda b, h, i: (b * nq + i, h)),
        out_shape=jax.ShapeDtypeStruct((t, N_HEADS * V_HEAD), BF16),
        compiler_params=_params("arbitrary", "arbitrary", "arbitrary"),
        name="attn",
    )(q, k, v)


def _s5_kernel(u_ref, bd_ref, ar_ref, ai_ref, cd_ref, dsk_ref, wglu_ref, bglu_ref, gain_ref,
               y_ref, bu_ref, st_ref, *, lc, bsz, cb):
    @pl.when(pl.program_id(0) == 0)
    def _():
        st_ref[...] = jnp.zeros_like(st_ref)

    u = u_ref[...]
    bu_ref[...] = _dot(u.astype(BF16), bd_ref[...])

    for c0 in range(0, N_STATE, cb):
        ar = ar_ref[:, c0:c0 + cb]
        ai = ai_ref[:, c0:c0 + cb]

        def step(t, carry, c0=c0, ar=ar, ai=ai):
            xr, xi = carry
            r0 = pl.multiple_of(t * bsz, bsz)
            bur = bu_ref[pl.ds(r0, bsz), c0:c0 + cb]
            bui = bu_ref[pl.ds(r0, bsz), N_STATE + c0:N_STATE + c0 + cb]
            nxr = ar * xr - ai * xi + bur
            nxi = ar * xi + ai * xr + bui
            bu_ref[pl.ds(r0, bsz), c0:c0 + cb] = nxr
            bu_ref[pl.ds(r0, bsz), N_STATE + c0:N_STATE + c0 + cb] = nxi
            return nxr, nxi

        xr0 = st_ref[:, c0:c0 + cb]
        xi0 = st_ref[:, N_STATE + c0:N_STATE + c0 + cb]
        xr, xi = lax.fori_loop(0, lc, step, (xr0, xi0))
        st_ref[:, c0:c0 + cb] = xr
        st_ref[:, N_STATE + c0:N_STATE + c0 + cb] = xi

    y = _dot(bu_ref[...].astype(BF16), cd_ref[...]) + dsk_ref[...] * u
    y = _gelu(y)
    y = y * jax.nn.sigmoid(_dot(y.astype(BF16), wglu_ref[...]) + bglu_ref[...])
    y_ref[...] = (_rms(y) * gain_ref[...]).astype(y_ref.dtype)


def _s5(u_tm, bd, ar, ai, cd, d_skip, w_glu, b_glu, gain, bsz, seq, lc):
    rows = lc * bsz
    consts = [bd, ar, ai, cd, d_skip.reshape(1, D_SSM), w_glu.astype(BF16), b_glu.reshape(1, D_SSM),
              gain.reshape(1, D_SSM)]
    return pl.pallas_call(
        functools.partial(_s5_kernel, lc=lc, bsz=bsz, cb=512),
        grid=(seq // lc,),
        in_specs=[pl.BlockSpec((rows, D_SSM), lambda j: (j, 0))] + [_const_spec(a.shape) for a in consts],
        out_specs=pl.BlockSpec((rows, D_SSM), lambda j: (j, 0)),
        out_shape=jax.ShapeDtypeStruct((seq * bsz, D_SSM), BF16),
        scratch_shapes=[pltpu.VMEM((rows, 2 * N_STATE), F32), pltpu.VMEM((bsz, 2 * N_STATE), F32)],
        compiler_params=_params("arbitrary"),
        name="s5",
    )(u_tm, *consts)


def _outproj_kernel(ys_ref, ya_ref, x_ref, gate_ref, scale_ref, shift_ref, ga_ref, wos_ref, woa_ref,
                    nffn_ref, wq_ref, x1_ref, h2_ref, qp_ref):
    ya = _rms(ya_ref[...].astype(F32)) * ga_ref[...]
    y = _dot(ys_ref[...], wos_ref[...]) + _dot(ya.astype(BF16), woa_ref[...])
    x1 = x_ref[...] + gate_ref[0] * y
    x1_ref[...] = x1
    h2 = _rms(x1) * nffn_ref[...]
    h2 = h2 * (1.0 + scale_ref[0]) + shift_ref[0]
    hi = h2.astype(BF16)
    lo = (h2 - hi.astype(F32)).astype(BF16)
    h2_ref[...] = jnp.concatenate(
        [part[:, c * 128:(c + 1) * 128] for part in (hi, lo) for c in _CHUNK_OF_PACKED], axis=1)
    qp = _dot(hi, wq_ref[...]).astype(qp_ref.dtype)
    for h in range(PEER_HEADS):
        qp_ref[h] = qp[:, h * 2 * PEER_HALF:(h + 1) * 2 * PEER_HALF]


def _outproj(ys_tm, ya, x2, gate_m, scale_f, shift_f, out_norm_attn, w_out, norm_ffn, w_query,
             bsz, seq, tm):
    t = bsz * seq
    nsb = seq // tm
    wo = w_out.astype(BF16)
    consts = [out_norm_attn.reshape(1, -1), wo[:D_SSM], wo[D_SSM:], norm_ffn.reshape(1, D_MODEL),
              w_query.astype(BF16)]
    mod_spec = pl.BlockSpec((1, 1, D_MODEL), lambda i: (i // nsb, 0, 0))
    row_spec = pl.BlockSpec((tm, D_MODEL), lambda i: (i, 0))
    return pl.pallas_call(
        _outproj_kernel,
        grid=(t // tm,),
        in_specs=[pl.BlockSpec((tm, D_SSM), lambda i: (i % nsb, i // nsb)),
                  pl.BlockSpec((tm, D_SSM), lambda i: (i, 0)),
                  row_spec, mod_spec, mod_spec, mod_spec] + [_const_spec(a.shape) for a in consts],
        out_specs=[row_spec, pl.BlockSpec((tm, 2 * D_MODEL), lambda i: (i, 0)),
                   pl.BlockSpec((PEER_HEADS, tm, 2 * PEER_HALF), lambda i: (0, i, 0))],
        out_shape=[jax.ShapeDtypeStruct((t, D_MODEL), F32), jax.ShapeDtypeStruct((t, 2 * D_MODEL), BF16),
                   jax.ShapeDtypeStruct((PEER_HEADS, t, 2 * PEER_HALF), BF16)],
        compiler_params=_params("arbitrary"),
        name="outproj",
    )(ys_tm, ya, x2, gate_m, scale_f, shift_f, *consts)


def _topk_rows(s, k, payload=None):
    n_rows = s.shape[0]
    iota = lax.broadcasted_iota(jnp.int32, s.shape, 0)
    vals, picks = [], []
    for _ in range(k):
        m = jnp.max(s, axis=0, keepdims=True)
        ix = jnp.min(jnp.where(s == m, iota, n_rows), axis=0, keepdims=True)
        hit = iota == ix
        vals.append(m)
        if payload is None:
            picks.append(ix)
        else:
            picks.append(jnp.sum(jnp.where(hit, payload, 0.0), axis=0, keepdims=True))
        s = jnp.where(hit, -jnp.inf, s)
    return jnp.concatenate(vals, axis=0), jnp.concatenate(picks, axis=0)


@functools.lru_cache(maxsize=None)
def _candidate_constants():
    k = PEER_TOPK
    pairs = [(a, b) for a in range(k) for b in range(k) if (a + 1) * (b + 1) <= k]
    rows = -(-len(pairs) // 16) * 16
    sel_a = np.zeros((rows, 128), np.float32)
    sel_b = np.zeros((rows, 128), np.float32)
    pad = np.zeros((rows, 1), np.float32)
    for r, (a, b) in enumerate(pairs):
        sel_a[r, a] = 1.0
        sel_b[r, b] = 1.0
    pad[len(pairs):] = -np.inf
    return sel_a, sel_b, pad


def _select_rows(sel, a):
    hi = a.astype(BF16)
    r1 = a - hi.astype(F32)
    mid = r1.astype(BF16)
    lo = (r1 - mid.astype(F32)).astype(BF16)
    return (_dot(sel, hi) + _dot(sel, mid)) + _dot(sel, lo)


def _topk_kernel(q_ref, keys_ref, sela_ref, selb_ref, pad_ref, idx_ref, g_ref):
    k = PEER_TOPK
    tm = q_ref.shape[1]
    fill = jnp.zeros((128 - k, tm), F32)

    def head(h, _):
        q = q_ref[h]
        s1 = _dot_nt(keys_ref[0], q[:, :PEER_HALF])
        s2 = _dot_nt(keys_ref[1], q[:, PEER_HALF:])
        v1, i1 = _topk_rows(s1, k)
        v2, i2 = _topk_rows(s2, k)
        sela = sela_ref[...]
        selb = selb_ref[...]
        cand = (_select_rows(sela, jnp.concatenate([v1, fill], axis=0))
                + _select_rows(selb, jnp.concatenate([v2, fill], axis=0))) + pad_ref[...]
        e1 = _dot(sela, jnp.concatenate([i1.astype(F32), fill], axis=0).astype(BF16))
        e2 = _dot(selb, jnp.concatenate([i2.astype(F32), fill], axis=0).astype(BF16))
        top_s, top_e = _topk_rows(cand, k, payload=e1 * PEER_KEYS + e2)
        idx_ref[h] = top_e.astype(jnp.int32) * _WORD_ROWS
        e = jnp.exp(top_s - top_s[0:1, :])
        g_ref[h] = e / jnp.sum(e, axis=0, keepdims=True)
        return 0

    lax.fori_loop(0, PEER_HEADS, head, 0)


def _topk(qp, sub_keys, tm):
    t = qp.shape[1]
    sel_a, sel_b, pad = _candidate_constants()
    consts = [sub_keys.astype(BF16), jnp.asarray(sel_a, BF16), jnp.asarray(sel_b, BF16), jnp.asarray(pad, F32)]
    return pl.pallas_call(
        _topk_kernel,
        grid=(t // tm,),
        in_specs=[pl.BlockSpec((PEER_HEADS, tm, 2 * PEER_HALF), lambda i: (0, i, 0))]
        + [_const_spec(a.shape) for a in consts],
        out_specs=[pl.BlockSpec((PEER_HEADS, PEER_TOPK, tm), lambda i: (0, 0, i))] * 2,
        out_shape=[jax.ShapeDtypeStruct((PEER_HEADS, PEER_TOPK, t), jnp.int32),
                   jax.ShapeDtypeStruct((PEER_HEADS, PEER_TOPK, t), F32)],
        compiler_params=_params("arbitrary"),
        name="topk",
    )(qp, *consts)


_CHUNK_OF_PACKED = tuple((q % 2) * 4 + q // 2 for q in range(8))
_WORD_ROWS = D_MODEL // 256


def _pack_table(w):
    e, d = w.shape
    wb = w.astype(BF16)
    pairs = jnp.stack([wb[:, :d // 2], wb[:, d // 2:]], axis=-1)
    return lax.bitcast_convert_type(pairs, jnp.int32).reshape(e * _WORD_ROWS, 128)


@functools.lru_cache(maxsize=None)
def _peer_constants():
    lane_q = np.arange(8 * N_SEL) % 8
    expand = np.zeros((N_SEL, 8 * N_SEL), np.float32)
    expand[np.arange(8 * N_SEL) // 8, np.arange(8 * N_SEL)] = 1.0
    msk_dn = (lane_q[None, :] == np.arange(8)[:, None]).astype(np.float32)
    packed_of_chunk = np.argsort(np.array(_CHUNK_OF_PACKED))
    msk_up = (lane_q[None, :] == packed_of_chunk[:, None]).astype(np.float32)
    return expand, msk_dn, msk_up


_GROUP = 8


def _gather_rows(win_ref, buf, q, tbl_ref, g_ref):
    for k in range(N_SEL):
        row = pl.multiple_of(win_ref[buf, q, k], _WORD_ROWS)
        g_ref[_WORD_ROWS * k:_WORD_ROWS * (k + 1), :] = tbl_ref[pl.ds(row, _WORD_ROWS), :]


def _windowed_tokens(tt, idx_hbm, win_ref, sem, stage, compute, finish):
    n_win = tt // _GROUP
    first = pl.program_id(0) * n_win
    last = idx_hbm.shape[0] - 1

    def window_copy(win, buf):
        src = idx_hbm.at[jnp.minimum(first + win, last)]
        return pltpu.make_async_copy(src, win_ref.at[buf], sem.at[buf])

    window_copy(0, 0).start()
    window_copy(0, 0).wait()
    window_copy(1, 1).start()
    stage(0, 0, 0, 0)
    stage(1, 0, 1, 1)

    def two_windows(i, _):
        base = 2 * _GROUP * i
        window_copy(2 * i + 1, 1).wait()
        for p in range(_GROUP):
            t0 = base + 2 * p
            r0 = compute(t0, 0)
            r1 = compute(t0 + 1, 1)
            if p == _GROUP - 1:
                window_copy(2 * i + 2, 0).wait()
            for slot in range(2):
                u = 2 * p + 2 + slot
                stage(jnp.minimum(base + u, tt - 1), (u // _GROUP) % 2, u % _GROUP, slot)
            if p == _GROUP // 2 - 2:
                window_copy(2 * i + 2, 0).start()
            if p == _GROUP - 2:
                window_copy(2 * i + 3, 1).start()
            finish(t0, r0)
            finish(t0 + 1, r1)
        return 0

    lax.fori_loop(0, n_win // 2, two_windows, 0)
    window_copy(n_win + 1, 1).wait()


def _peer_dn_kernel(idx_hbm, hm_ref, tbl_ref, msk_ref, et_ref, a_ref, win_ref, sem, y_ref, *g_refs, tt):
    def stage(t, buf, q, slot):
        _gather_rows(win_ref, buf, q, tbl_ref, g_refs[slot])

    def compute(t, slot):
        rows = pltpu.bitcast(g_refs[slot][...], BF16)
        return _dot_nt(hm_ref[t], rows)

    def finish(t, y):
        y_ref[t] = y[:8] + y[8:]

    _windowed_tokens(tt, idx_hbm, win_ref, sem, stage, compute, finish)
    s = jnp.sum(y_ref[...] * msk_ref[...], axis=1)
    a_ref[...] = _dot_split(s, et_ref[...])


def _peer_dn(idx, hm, tbl, tt):
    t = hm.shape[0]
    expand, msk_dn, _ = _peer_constants()
    consts = [jnp.asarray(msk_dn, F32), jnp.asarray(expand.T.copy(), BF16)]
    return pl.pallas_call(
        functools.partial(_peer_dn_kernel, tt=tt),
        grid=(t // tt,),
        in_specs=[pl.BlockSpec(memory_space=pl.ANY),
                  pl.BlockSpec((tt, 16, 128), lambda i: (i, 0, 0)),
                  pl.BlockSpec(memory_space=pltpu.VMEM)] + [_const_spec(a.shape) for a in consts],
        out_specs=pl.BlockSpec((tt, N_SEL), lambda i: (i, 0)),
        out_shape=jax.ShapeDtypeStruct((t, N_SEL), F32),
        scratch_shapes=[pltpu.SMEM((2, _GROUP, N_SEL), jnp.int32), pltpu.SemaphoreType.DMA((2,)),
                        pltpu.VMEM((tt, 8, 8 * N_SEL), F32)] + _gather_buffers(),
        compiler_params=_params("arbitrary"),
        name="peer_dn",
    )(idx, hm, tbl, *consts)


def _gate_kernel(a_ref, g_ref, w_ref):
    w_ref[...] = g_ref[...] * _gelu(a_ref[...])


def _gate(a, g):
    t = a.shape[0]
    blk = min(t, 2048)
    spec = pl.BlockSpec((blk, N_SEL), lambda i: (i, 0))
    return pl.pallas_call(
        _gate_kernel, grid=(t // blk,), in_specs=[spec, spec], out_specs=spec,
        out_shape=jax.ShapeDtypeStruct((t, N_SEL), F32),
        compiler_params=_params("arbitrary"), name="gate",
    )(a, g)


def _gather_buffers():
    return [pltpu.VMEM((_WORD_ROWS * N_SEL, 128), jnp.int32) for _ in range(2)]


def _peer_up_kernel(idx_hbm, w_ref, x1_ref, gate_ref, tbl_ref, e_ref, msk_ref, o_ref,
                    win_ref, sem, wexp_ref, lhs_ref, *g_refs, tt):
    wexp_ref[...] = _dot_split(w_ref[...], e_ref[...])
    gate = gate_ref[0]
    msk = msk_ref[...]
    top = lax.broadcasted_iota(jnp.int32, msk.shape, 0) < 8

    def stage(t, buf, q, slot):
        _gather_rows(win_ref, buf, q, tbl_ref, g_refs[slot])
        wsel = wexp_ref[pl.ds(t, 1), :] * msk
        hi = wsel.astype(BF16)
        lo = (wsel - hi.astype(F32)).astype(BF16)
        lhs_ref[slot] = jnp.where(top, hi, lo)

    def compute(t, slot):
        rows = pltpu.bitcast(g_refs[slot][...], BF16)
        return _dot(lhs_ref[slot], rows)

    def finish(t, o16):
        o_ref[t] = x1_ref[t] + gate * (o16[:8] + o16[8:])

    _windowed_tokens(tt, idx_hbm, win_ref, sem, stage, compute, finish)


def _peer_up(idx, wgt, x13, gate3, tbl, seq, tt):
    t = x13.shape[0]
    nsb = seq // tt
    expand, _, msk_up = _peer_constants()
    consts = [jnp.asarray(expand, BF16), jnp.asarray(np.concatenate([msk_up, msk_up], axis=0), F32)]
    row = pl.BlockSpec((tt, 8, 128), lambda i: (i, 0, 0))
    return pl.pallas_call(
        functools.partial(_peer_up_kernel, tt=tt),
        grid=(t // tt,),
        in_specs=[pl.BlockSpec(memory_space=pl.ANY),
                  pl.BlockSpec((tt, N_SEL), lambda i: (i, 0)),
                  row, pl.BlockSpec((1, 8, 128), lambda i: (i // nsb, 0, 0)),
                  pl.BlockSpec(memory_space=pltpu.VMEM)] + [_const_spec(a.shape) for a in consts],
        out_specs=row,
        out_shape=jax.ShapeDtypeStruct((t, 8, 128), F32),
        scratch_shapes=[pltpu.SMEM((2, _GROUP, N_SEL), jnp.int32), pltpu.SemaphoreType.DMA((2,)),
                        pltpu.VMEM((tt, 8 * N_SEL), F32), pltpu.VMEM((2, 16, 8 * N_SEL), BF16)]
        + _gather_buffers(),
        compiler_params=_params("arbitrary"),
        name="peer_up",
    )(idx, wgt, x13, gate3, tbl, *consts)


def _layer(x, c, positions, w_ada, b_ada, norm_mix, norm_ffn, w_in, lam_re, lam_im, log_dt, b_re, b_im,
           c_re, c_im, d_skip, w_glu, b_glu, q_a_norm, w_uq, kv_a_norm, w_ukv, q_norm, k_norm,
           out_norm_ssm, out_norm_attn, w_out, w_query, sub_keys, expert_down, expert_up):
    bsz, seq, _ = x.shape
    t = bsz * seq
    tm = min(256, seq)
    x2 = x.reshape(t, D_MODEL)

    mod = _ada(c, w_ada, b_ada)
    shift_m, scale_m, gate_m, shift_f, scale_f, gate_f = [
        m.reshape(bsz, 1, D_MODEL) for m in jnp.split(mod, N_ADA, axis=-1)]

    cs = _rope_tables(positions)
    u_tm, q, k, v = _inproj(x2, scale_m, shift_m, norm_mix, w_in, q_a_norm, w_uq, kv_a_norm, w_ukv,
                            q_norm, k_norm, cs, bsz, seq, tm)
    ya = _attention(q, k, v, bsz, seq, tm)

    ar, ai, bbr, bbi = _s5_params(lam_re, lam_im, log_dt, b_re, b_im)
    bd = jnp.concatenate([_block_diag(bbr), _block_diag(bbi)], axis=1).astype(BF16)
    cd = jnp.concatenate([_block_diag(jnp.transpose(c_re, (0, 2, 1))),
                          _block_diag(jnp.transpose(-c_im, (0, 2, 1)))], axis=0).astype(BF16)
    ys_tm = _s5(u_tm.reshape(seq * bsz, D_SSM), bd, ar.reshape(1, N_STATE), ai.reshape(1, N_STATE), cd,
                d_skip, w_glu, b_glu, out_norm_ssm, bsz, seq, min(32, seq))

    x1, hm, qp = _outproj(ys_tm.reshape(seq, bsz * D_SSM), ya, x2, gate_m, scale_f, shift_f,
                          out_norm_attn, w_out, norm_ffn, w_query, bsz, seq, tm)

    idx_t, g_t = _topk(qp, sub_keys, tm)
    idx = jnp.transpose(idx_t.reshape(N_SEL, t))
    g = jnp.transpose(g_t.reshape(N_SEL, t))
    idx_win = jnp.pad(idx, ((0, _GROUP), (0, 0))).reshape(t // _GROUP + 1, _GROUP, N_SEL)

    tt = min(128, seq)
    a = _peer_dn(idx_win, hm.reshape(t, 16, 128), _pack_table(expert_down), tt)
    wgt = _gate(a, g)
    out = _peer_up(idx_win, wgt, x1.reshape(t, 8, 128), gate_f.reshape(bsz, 8, 128),
                   _pack_table(expert_up), seq, tt)
    return out.reshape(bsz, seq, D_MODEL)


def kernel(x, c, positions, w_ada, b_ada, norm_mix, norm_ffn, w_in, lam_re, lam_im, log_dt, b_re, b_im, c_re, c_im, d_skip, w_glu, b_glu, q_a_norm, w_uq, kv_a_norm, w_ukv, q_norm, k_norm, out_norm_ssm, out_norm_attn, w_out, w_query, sub_keys, expert_down, expert_up):
    for l in range(w_ada.shape[0]):
        x = _layer(x, c, positions, w_ada[l], b_ada[l], norm_mix[l], norm_ffn[l], w_in[l],
                   lam_re[l], lam_im[l], log_dt[l], b_re[l], b_im[l], c_re[l], c_im[l],
                   d_skip[l], w_glu[l], b_glu[l], q_a_norm[l], w_uq[l], kv_a_norm[l], w_ukv[l],
                   q_norm[l], k_norm[l], out_norm_ssm[l], out_norm_attn[l], w_out[l],
                   w_query[l], sub_keys[l], expert_down[l], expert_up[l])
    return x
```

```python
import functools
import math

import numpy as np
import jax
import jax.numpy as jnp
from jax import lax
from jax.experimental import pallas as pl
from jax.experimental.pallas import tpu as pltpu

F32 = jnp.float32
BF16 = jnp.bfloat16

D_MODEL = 1024
D_SSM = 512
SSM_GROUP = 16
N_SSM_GROUPS = 32
SSM_STATE = 64
N_STATE = N_SSM_GROUPS * SSM_STATE
N_HEADS = 8
QK_NOPE = 64
QK_ROPE = 32
QK_HEAD = 96
V_HEAD = 64
HEAD_PAD = 128
Q_LORA = 256
KV_LORA = 128
ROPE_THETA = 10000.0
PEER_HEADS = 8
PEER_KEYS = 128
PEER_TOPK = 16
PEER_HALF = 128
N_SEL = PEER_HEADS * PEER_TOPK
N_ADA = 6
EPS = 1e-6
GELU_C = math.sqrt(2.0 / math.pi)

VMEM_LIMIT = 48 * 1024 * 1024


def _dot(a, b):
    return jnp.dot(a, b, preferred_element_type=F32)


def _dot_nt(a, b):
    return lax.dot_general(a, b, (((1,), (1,)), ((), ())), preferred_element_type=F32)


def _dot_split(a, sel):
    hi = a.astype(BF16)
    lo = (a - hi.astype(F32)).astype(BF16)
    return _dot(hi, sel) + _dot(lo, sel)


def _rms(x):
    return x * lax.rsqrt(jnp.mean(x * x, axis=-1, keepdims=True) + EPS)


def _gelu(x):
    return 0.5 * x * (1.0 + jnp.tanh(GELU_C * (x + 0.044715 * x * x * x)))


def _params(*sem):
    return pltpu.CompilerParams(dimension_semantics=sem, vmem_limit_bytes=VMEM_LIMIT)


def _const_spec(shape):
    nd = len(shape)
    return pl.BlockSpec(shape, lambda *_: (0,) * nd)


def _ada_kernel(c_ref, w_ref, b_ref, o_ref):
    c = c_ref[...]
    s = c * jax.nn.sigmoid(c)
    o_ref[...] = _dot_split2(s, w_ref[...]) + b_ref[...]


def _dot_split2(a, w):
    ah = a.astype(BF16)
    al = (a - ah.astype(F32)).astype(BF16)
    wh = w.astype(BF16)
    wl = (w - wh.astype(F32)).astype(BF16)
    return _dot(ah, wh) + (_dot(ah, wl) + _dot(al, wh))


def _ada(c, w_ada, b_ada):
    bsz = c.shape[0]
    n = w_ada.shape[1]
    blk = D_MODEL
    return pl.pallas_call(
        _ada_kernel,
        grid=(n // blk,),
        in_specs=[_const_spec((bsz, D_MODEL)),
                  pl.BlockSpec((D_MODEL, blk), lambda j: (0, j)),
                  pl.BlockSpec((1, blk), lambda j: (0, j))],
        out_specs=pl.BlockSpec((bsz, blk), lambda j: (0, j)),
        out_shape=jax.ShapeDtypeStruct((bsz, n), F32),
        compiler_params=_params("arbitrary"),
        name="ada",
    )(c, w_ada, b_ada.reshape(1, n))


def _rope_kernel(pos_ref, freq_ref, cos_ref, sin_ref):
    ang = pos_ref[...].astype(F32) * freq_ref[...]
    cos_ref[...] = jnp.cos(ang)
    sin_ref[...] = jnp.sin(ang)


def _rope_tables(positions):
    half = QK_ROPE // 2
    t = positions.size
    rows = t * half // 128
    pos_rep = jnp.repeat(positions.reshape(-1), half).reshape(rows, 128)
    inv_freq = ROPE_THETA ** (-jnp.arange(half, dtype=F32) / half)
    freq_row = jnp.tile(inv_freq, 128 // half).reshape(1, 128)
    blk = min(rows, 512)
    cos_d, sin_d = pl.pallas_call(
        _rope_kernel,
        grid=(rows // blk,),
        in_specs=[pl.BlockSpec((blk, 128), lambda i: (i, 0)), _const_spec((1, 128))],
        out_specs=[pl.BlockSpec((blk, 128), lambda i: (i, 0))] * 2,
        out_shape=[jax.ShapeDtypeStruct((rows, 128), F32)] * 2,
        compiler_params=_params("arbitrary"),
        name="rope",
    )(pos_rep, freq_row)
    return jnp.concatenate(
        [cos_d.reshape(t, half), sin_d.reshape(t, half), jnp.zeros((t, 128 - 2 * half), F32)], axis=1)


def _s5par_kernel(lr_ref, li_ref, ldt_ref, bre_ref, bim_ref, ar_ref, ai_ref, bbr_ref, bbi_ref):
    lr = lr_ref[...]
    li = li_ref[...]
    dt = jnp.exp(ldt_ref[...])
    mag = jnp.exp(lr * dt)
    ar = mag * jnp.cos(li * dt)
    ai = mag * jnp.sin(li * dt)
    den = lr * lr + li * li
    nr = ar - 1.0
    ni = ai
    coef_r = (nr * lr + ni * li) / den
    coef_i = (ni * lr - nr * li) / den
    ar_ref[...] = ar
    ai_ref[...] = ai
    bre = bre_ref[...]
    bim = bim_ref[...]
    cr = coef_r[:, None, :]
    ci = coef_i[:, None, :]
    bbr_ref[...] = cr * bre - ci * bim
    bbi_ref[...] = cr * bim + ci * bre


def _s5_params(lam_re, lam_im, log_dt, b_re, b_im):
    g, p = lam_re.shape
    c = b_re.shape[-1]
    bre_t = jnp.transpose(b_re, (0, 2, 1))
    bim_t = jnp.transpose(b_im, (0, 2, 1))
    return pl.pallas_call(
        _s5par_kernel,
        out_shape=[jax.ShapeDtypeStruct((g, p), F32), jax.ShapeDtypeStruct((g, p), F32),
                   jax.ShapeDtypeStruct((g, c, p), F32), jax.ShapeDtypeStruct((g, c, p), F32)],
        name="s5par",
    )(lam_re, lam_im, log_dt.reshape(g, 1), bre_t, bim_t)


def _block_diag(blocks):
    g, r, c = blocks.shape
    eye = jnp.eye(g, dtype=blocks.dtype)
    return (blocks[:, :, None, :] * eye[:, None, :, None]).reshape(g * r, g * c)


@functools.lru_cache(maxsize=None)
def _layout_constants():
    hp, nh = HEAD_PAD, N_HEADS
    width = nh * hp
    half = QK_ROPE // 2
    place = np.zeros((128, width), np.float32)
    hsel = np.zeros((width, 128), np.float32)
    rot = np.zeros((width, width), np.float32)
    ecos = np.zeros((128, width), np.float32)
    esin = np.zeros((128, width), np.float32)
    for h in range(nh):
        base = h * hp
        hsel[base:base + QK_HEAD, h] = 1.0
        for j in range(QK_ROPE):
            place[j, base + QK_NOPE + j] = 1.0
        for j in range(half):
            c1 = base + QK_NOPE + j
            c2 = c1 + half
            rot[c2, c1] = -1.0
            rot[c1, c2] = 1.0
            ecos[j, c1] = 1.0
            ecos[j, c2] = 1.0
            esin[half + j, c1] = 1.0
            esin[half + j, c2] = 1.0
    ones_nope = (ecos.sum(axis=0, keepdims=True) == 0).astype(np.float32)
    return place, hsel, rot, ecos, esin, ones_nope


def _pad_heads(w, head_dim):
    k = w.shape[0]
    w = w.reshape(k, N_HEADS, head_dim)
    w = jnp.pad(w, ((0, 0), (0, 0), (0, HEAD_PAD - head_dim)))
    return w.reshape(k, N_HEADS * HEAD_PAD)


def _inproj_kernel(x_ref, scale_ref, shift_ref, nmix_ref, win_ref, qan_ref, wuq_ref, kvan_ref,
                   wk_ref, wv_ref, place_ref, hsel_ref, hselt_ref, qg_ref, kg_ref, rot_ref,
                   ecos_ref, esin_ref, nope_ref, cs_ref,
                   u_ref, q_ref, k_ref, v_ref):
    x = x_ref[...]
    h = _rms(x) * nmix_ref[...]
    h = h * (1.0 + scale_ref[0]) + shift_ref[0]
    proj = _dot(h.astype(BF16), win_ref[...])
    u_ref[...] = proj[:, :D_SSM]
    o1 = D_SSM + Q_LORA
    o2 = o1 + KV_LORA
    cq = _rms(proj[:, D_SSM:o1]) * qan_ref[...]
    ckv = (_rms(proj[:, o1:o2]) * kvan_ref[...]).astype(BF16)
    kr = proj[:, o2:]
    q = _dot(cq.astype(BF16), wuq_ref[...])
    k = _dot(ckv, wk_ref[...]) + _dot_split(kr, place_ref[...])
    v_ref[...] = _dot(ckv, wv_ref[...]).astype(v_ref.dtype)

    cs = cs_ref[...]
    cos = _dot_split(cs, ecos_ref[...]) + nope_ref[...]
    sin = _dot_split(cs, esin_ref[...])

    def head_norm_rope(z, gain):
        ssq = _dot_split(z * z, hsel_ref[...])
        r = lax.rsqrt(ssq * (1.0 / QK_HEAD) + EPS)
        zn = z * _dot_split(r, hselt_ref[...]) * gain
        return zn * cos + _dot(zn.astype(BF16), rot_ref[...]) * sin

    qr = head_norm_rope(q, qg_ref[...]) * (QK_HEAD ** -0.5)
    q_ref[...] = qr.astype(q_ref.dtype)
    k_ref[...] = head_norm_rope(k, kg_ref[...]).astype(k_ref.dtype)


def _inproj(x2, scale_m, shift_m, norm_mix, w_in, q_a_norm, w_uq, kv_a_norm, w_ukv, q_norm, k_norm,
            cs, bsz, seq, tm):
    t = bsz * seq
    nsb = seq // tm
    place, hsel, rot, ecos, esin, ones_nope = _layout_constants()
    width = N_HEADS * HEAD_PAD
    win = jnp.pad(w_in, ((0, 0), (0, D_MODEL - w_in.shape[1]))).astype(BF16)
    wuq = _pad_heads(w_uq, QK_HEAD).astype(BF16)
    wkv = w_ukv.reshape(KV_LORA, N_HEADS, QK_NOPE + V_HEAD)
    wk = _pad_heads(wkv[:, :, :QK_NOPE].reshape(KV_LORA, N_HEADS * QK_NOPE), QK_NOPE).astype(BF16)
    wv = wkv[:, :, QK_NOPE:].reshape(KV_LORA, N_HEADS * V_HEAD).astype(BF16)
    qg = _pad_heads(jnp.tile(q_norm, N_HEADS).reshape(1, -1), QK_HEAD)
    kg = _pad_heads(jnp.tile(k_norm, N_HEADS).reshape(1, -1), QK_HEAD)
    consts = [
        norm_mix.reshape(1, D_MODEL), win, q_a_norm.reshape(1, Q_LORA), wuq,
        kv_a_norm.reshape(1, KV_LORA), wk, wv,
        jnp.asarray(place, BF16), jnp.asarray(hsel, BF16), jnp.asarray(hsel.T.copy(), BF16), qg, kg,
        jnp.asarray(rot, BF16), jnp.asarray(ecos, BF16), jnp.asarray(esin, BF16),
        jnp.asarray(ones_nope, F32),
    ]
    mod_spec = pl.BlockSpec((1, 1, D_MODEL), lambda i: (i // nsb, 0, 0))
    in_specs = ([pl.BlockSpec((tm, D_MODEL), lambda i: (i, 0)), mod_spec, mod_spec]
                + [_const_spec(a.shape) for a in consts]
                + [pl.BlockSpec((tm, 128), lambda i: (i, 0))])
    out_specs = [
        pl.BlockSpec((tm, D_SSM), lambda i: (i % nsb, i // nsb)),
        pl.BlockSpec((tm, width), lambda i: (i, 0)),
        pl.BlockSpec((tm, width), lambda i: (i, 0)),
        pl.BlockSpec((tm, N_HEADS * V_HEAD), lambda i: (i, 0)),
    ]
    out_shape = [
        jax.ShapeDtypeStruct((seq, bsz * D_SSM), F32),
        jax.ShapeDtypeStruct((t, width), BF16),
        jax.ShapeDtypeStruct((t, width), BF16),
        jax.ShapeDtypeStruct((t, N_HEADS * V_HEAD), BF16),
    ]
    return pl.pallas_call(
        _inproj_kernel, grid=(t // tm,), in_specs=in_specs, out_specs=out_specs, out_shape=out_shape,
        compiler_params=_params("arbitrary"), name="inproj",
    )(x2, scale_m, shift_m, *consts, cs)


def _attn_kernel(q_ref, k_ref, v_ref, o_ref, *, tq, rows, pairs):
    qi = pl.program_id(2)
    lane = lax.broadcasted_iota(jnp.int32, (1, 2 * V_HEAD), 1)
    first = lane < V_HEAD
    groups = tq // rows

    def block(j, carry, masked):
        r0 = pl.multiple_of(j * tq, tq)
        out = []
        for hp in range(pairs):
            kb = k_ref[pl.ds(r0, tq), hp * 2 * HEAD_PAD:(hp + 1) * 2 * HEAD_PAD]
            vb = v_ref[pl.ds(r0, tq), hp * 2 * V_HEAD:(hp + 1) * 2 * V_HEAD]
            one = jnp.ones_like(vb)
            v0 = jnp.where(first, vb, one)
            v1 = jnp.where(first, one, vb)
            for rg in range(groups):
                m0, m1, acc0, acc1 = carry[hp * groups + rg]
                q0 = q_ref[rg * rows:(rg + 1) * rows, hp * 2 * HEAD_PAD:hp * 2 * HEAD_PAD + HEAD_PAD]
                q1 = q_ref[rg * rows:(rg + 1) * rows, hp * 2 * HEAD_PAD + HEAD_PAD:(hp + 1) * 2 * HEAD_PAD]
                s0 = _dot_nt(q0, kb[:, :HEAD_PAD])
                s1 = _dot_nt(q1, kb[:, HEAD_PAD:])
                if masked:
                    row = lax.broadcasted_iota(jnp.int32, (rows, tq), 0) + rg * rows
                    col = lax.broadcasted_iota(jnp.int32, (rows, tq), 1)
                    keep = col <= row
                    s0 = jnp.where(keep, s0, -jnp.inf)
                    s1 = jnp.where(keep, s1, -jnp.inf)
                n0 = jnp.maximum(m0, jnp.max(s0, axis=-1, keepdims=True))
                n1 = jnp.maximum(m1, jnp.max(s1, axis=-1, keepdims=True))
                p0 = jnp.exp(s0 - n0).astype(BF16)
                p1 = jnp.exp(s1 - n1).astype(BF16)
                acc0 = jnp.exp(m0 - n0) * acc0 + _dot(p0, v0)
                acc1 = jnp.exp(m1 - n1) * acc1 + _dot(p1, v1)
                out.append((n0, n1, acc0, acc1))
        return tuple(out)

    neg = jnp.full((rows, 1), -jnp.inf, F32)
    zacc = jnp.zeros((rows, 2 * V_HEAD), F32)
    carry = lax.fori_loop(0, qi, lambda j, c: block(j, c, False), ((neg, neg, zacc, zacc),) * (pairs * groups))
    final = block(qi, carry, True)
    for hp in range(pairs):
        for rg in range(groups):
            _, _, acc0, acc1 = final[hp * groups + rg]
            out0 = acc0 * pltpu.roll(1.0 / acc0, V_HEAD, axis=1)
            out1 = acc1 * pltpu.roll(1.0 / acc1, V_HEAD, axis=1)
            o_ref[rg * rows:(rg + 1) * rows, hp * 2 * V_HEAD:(hp + 1) * 2 * V_HEAD] = (
                jnp.where(first, out0, out1).astype(o_ref.dtype))


def _attention(q, k, v, bsz, seq, tq, pairs=1):
    nq = seq // tq
    t = bsz * seq
    return pl.pallas_call(
        functools.partial(_attn_kernel, tq=tq, rows=tq, pairs=pairs),
        grid=(bsz, N_HEADS // (2 * pairs), nq),
        in_specs=[pl.BlockSpec((tq, pairs * 2 * HEAD_PAD), lambda b, h, i: (b * nq + i, h)),
                  pl.BlockSpec((seq, pairs * 2 * HEAD_PAD), lambda b, h, i: (b, h)),
                  pl.BlockSpec((seq, pairs * 2 * V_HEAD), lambda b, h, i: (b, h))],
        out_specs=pl.BlockSpec((tq, pairs * 2 * V_HEAD), lambda b, h, i: (b * nq + i, h)),
        out_shape=jax.ShapeDtypeStruct((t, N_HEADS * V_HEAD), BF16),
        compiler_params=_params("arbitrary", "arbitrary", "arbitrary"),
        name="attn",
    )(q, k, v)


def _s5_kernel(u_ref, bd_ref, ar_ref, ai_ref, cd_ref, dsk_ref, wglu_ref, bglu_ref, gain_ref,
               y_ref, bu_ref, st_ref, *, lc, bsz, cb):
    @pl.when(pl.program_id(0) == 0)
    def _():
        st_ref[...] = jnp.zeros_like(st_ref)

    u = u_ref[...]
    bu_ref[...] = _dot(u.astype(BF16), bd_ref[...])

    for c0 in range(0, N_STATE, cb):
        ar = ar_ref[:, c0:c0 + cb]
        ai = ai_ref[:, c0:c0 + cb]

        def step(t, carry, c0=c0, ar=ar, ai=ai):
            xr, xi = carry
            r0 = pl.multiple_of(t * bsz, bsz)
            bur = bu_ref[pl.ds(r0, bsz), c0:c0 + cb]
            bui = bu_ref[pl.ds(r0, bsz), N_STATE + c0:N_STATE + c0 + cb]
            nxr = ar * xr - ai * xi + bur
            nxi = ar * xi + ai * xr + bui
            bu_ref[pl.ds(r0, bsz), c0:c0 + cb] = nxr
            bu_ref[pl.ds(r0, bsz), N_STATE + c0:N_STATE + c0 + cb] = nxi
            return nxr, nxi

        xr0 = st_ref[:, c0:c0 + cb]
        xi0 = st_ref[:, N_STATE + c0:N_STATE + c0 + cb]
        xr, xi = lax.fori_loop(0, lc, step, (xr0, xi0))
        st_ref[:, c0:c0 + cb] = xr
        st_ref[:, N_STATE + c0:N_STATE + c0 + cb] = xi

    y = _dot(bu_ref[...].astype(BF16), cd_ref[...]) + dsk_ref[...] * u
    y = _gelu(y)
    y = y * jax.nn.sigmoid(_dot(y.astype(BF16), wglu_ref[...]) + bglu_ref[...])
    y_ref[...] = (_rms(y) * gain_ref[...]).astype(y_ref.dtype)


def _s5(u_tm, bd, ar, ai, cd, d_skip, w_glu, b_glu, gain, bsz, seq, lc):
    rows = lc * bsz
    consts = [bd, ar, ai, cd, d_skip.reshape(1, D_SSM), w_glu.astype(BF16), b_glu.reshape(1, D_SSM),
              gain.reshape(1, D_SSM)]
    return pl.pallas_call(
        functools.partial(_s5_kernel, lc=lc, bsz=bsz, cb=512),
        grid=(seq // lc,),
        in_specs=[pl.BlockSpec((rows, D_SSM), lambda j: (j, 0))] + [_const_spec(a.shape) for a in consts],
        out_specs=pl.BlockSpec((rows, D_SSM), lambda j: (j, 0)),
        out_shape=jax.ShapeDtypeStruct((seq * bsz, D_SSM), BF16),
        scratch_shapes=[pltpu.VMEM((rows, 2 * N_STATE), F32), pltpu.VMEM((bsz, 2 * N_STATE), F32)],
        compiler_params=_params("arbitrary"),
        name="s5",
    )(u_tm, *consts)


def _outproj_kernel(ys_ref, ya_ref, x_ref, gate_ref, scale_ref, shift_ref, ga_ref, wos_ref, woa_ref,
                    nffn_ref, wq_ref, x1_ref, h2_ref, qp_ref):
    ya = _rms(ya_ref[...].astype(F32)) * ga_ref[...]
    y = _dot(ys_ref[...], wos_ref[...]) + _dot(ya.astype(BF16), woa_ref[...])
    x1 = x_ref[...] + gate_ref[0] * y
    x1_ref[...] = x1
    h2 = _rms(x1) * nffn_ref[...]
    h2 = h2 * (1.0 + scale_ref[0]) + shift_ref[0]
    hi = h2.astype(BF16)
    lo = (h2 - hi.astype(F32)).astype(BF16)
    h2_ref[...] = jnp.concatenate(
        [part[:, c * 128:(c + 1) * 128] for part in (hi, lo) for c in _CHUNK_OF_PACKED], axis=1)
    qp = _dot(hi, wq_ref[...]).astype(qp_ref.dtype)
    for h in range(PEER_HEADS):
        qp_ref[h] = qp[:, h * 2 * PEER_HALF:(h + 1) * 2 * PEER_HALF]


def _outproj(ys_tm, ya, x2, gate_m, scale_f, shift_f, out_norm_attn, w_out, norm_ffn, w_query,
             bsz, seq, tm):
    t = bsz * seq
    nsb = seq // tm
    wo = w_out.astype(BF16)
    consts = [out_norm_attn.reshape(1, -1), wo[:D_SSM], wo[D_SSM:], norm_ffn.reshape(1, D_MODEL),
              w_query.astype(BF16)]
    mod_spec = pl.BlockSpec((1, 1, D_MODEL), lambda i: (i // nsb, 0, 0))
    row_spec = pl.BlockSpec((tm, D_MODEL), lambda i: (i, 0))
    return pl.pallas_call(
        _outproj_kernel,
        grid=(t // tm,),
        in_specs=[pl.BlockSpec((tm, D_SSM), lambda i: (i % nsb, i // nsb)),
                  pl.BlockSpec((tm, D_SSM), lambda i: (i, 0)),
                  row_spec, mod_spec, mod_spec, mod_spec] + [_const_spec(a.shape) for a in consts],
        out_specs=[row_spec, pl.BlockSpec((tm, 2 * D_MODEL), lambda i: (i, 0)),
                   pl.BlockSpec((PEER_HEADS, tm, 2 * PEER_HALF), lambda i: (0, i, 0))],
        out_shape=[jax.ShapeDtypeStruct((t, D_MODEL), F32), jax.ShapeDtypeStruct((t, 2 * D_MODEL), BF16),
                   jax.ShapeDtypeStruct((PEER_HEADS, t, 2 * PEER_HALF), BF16)],
        compiler_params=_params("arbitrary"),
        name="outproj",
    )(ys_tm, ya, x2, gate_m, scale_f, shift_f, *consts)


def _topk_rows(s, k, payload=None):
    n_rows = s.shape[0]
    iota = lax.broadcasted_iota(jnp.int32, s.shape, 0)
    vals, picks = [], []
    for _ in range(k):
        m = jnp.max(s, axis=0, keepdims=True)
        ix = jnp.min(jnp.where(s == m, iota, n_rows), axis=0, keepdims=True)
        hit = iota == ix
        vals.append(m)
        if payload is None:
            picks.append(ix)
        else:
            picks.append(jnp.sum(jnp.where(hit, payload, 0.0), axis=0, keepdims=True))
        s = jnp.where(hit, -jnp.inf, s)
    return jnp.concatenate(vals, axis=0), jnp.concatenate(picks, axis=0)


@functools.lru_cache(maxsize=None)
def _candidate_constants():
    k = PEER_TOPK
    pairs = [(a, b) for a in range(k) for b in range(k) if (a + 1) * (b + 1) <= k]
    rows = -(-len(pairs) // 16) * 16
    sel_a = np.zeros((rows, 128), np.float32)
    sel_b = np.zeros((rows, 128), np.float32)
    pad = np.zeros((rows, 1), np.float32)
    for r, (a, b) in enumerate(pairs):
        sel_a[r, a] = 1.0
        sel_b[r, b] = 1.0
    pad[len(pairs):] = -np.inf
    return sel_a, sel_b, pad


def _select_rows(sel, a):
    hi = a.astype(BF16)
    r1 = a - hi.astype(F32)
    mid = r1.astype(BF16)
    lo = (r1 - mid.astype(F32)).astype(BF16)
    return (_dot(sel, hi) + _dot(sel, mid)) + _dot(sel, lo)


def _topk_kernel(q_ref, keys_ref, sela_ref, selb_ref, pad_ref, idx_ref, g_ref):
    k = PEER_TOPK
    tm = q_ref.shape[1]
    fill = jnp.zeros((128 - k, tm), F32)

    def head(h, _):
        q = q_ref[h]
        s1 = _dot_nt(keys_ref[0], q[:, :PEER_HALF])
        s2 = _dot_nt(keys_ref[1], q[:, PEER_HALF:])
        v1, i1 = _topk_rows(s1, k)
        v2, i2 = _topk_rows(s2, k)
        sela = sela_ref[...]
        selb = selb_ref[...]
        cand = (_select_rows(sela, jnp.concatenate([v1, fill], axis=0))
                + _select_rows(selb, jnp.concatenate([v2, fill], axis=0))) + pad_ref[...]
        e1 = _dot(sela, jnp.concatenate([i1.astype(F32), fill], axis=0).astype(BF16))
        e2 = _dot(selb, jnp.concatenate([i2.astype(F32), fill], axis=0).astype(BF16))
        top_s, top_e = _topk_rows(cand, k, payload=e1 * PEER_KEYS + e2)
        idx_ref[h] = top_e.astype(jnp.int32) * _WORD_ROWS
        e = jnp.exp(top_s - top_s[0:1, :])
        g_ref[h] = e / jnp.sum(e, axis=0, keepdims=True)
        return 0

    lax.fori_loop(0, PEER_HEADS, head, 0)


def _topk(qp, sub_keys, tm):
    t = qp.shape[1]
    sel_a, sel_b, pad = _candidate_constants()
    consts = [sub_keys.astype(BF16), jnp.asarray(sel_a, BF16), jnp.asarray(sel_b, BF16), jnp.asarray(pad, F32)]
    return pl.pallas_call(
        _topk_kernel,
        grid=(t // tm,),
        in_specs=[pl.BlockSpec((PEER_HEADS, tm, 2 * PEER_HALF), lambda i: (0, i, 0))]
        + [_const_spec(a.shape) for a in consts],
        out_specs=[pl.BlockSpec((PEER_HEADS, PEER_TOPK, tm), lambda i: (0, 0, i))] * 2,
        out_shape=[jax.ShapeDtypeStruct((PEER_HEADS, PEER_TOPK, t), jnp.int32),
                   jax.ShapeDtypeStruct((PEER_HEADS, PEER_TOPK, t), F32)],
        compiler_params=_params("arbitrary"),
        name="topk",
    )(qp, *consts)


_CHUNK_OF_PACKED = tuple((q % 2) * 4 + q // 2 for q in range(8))
_WORD_ROWS = D_MODEL // 256


def _pack_table(w):
    e, d = w.shape
    wb = w.astype(BF16)
    pairs = jnp.stack([wb[:, :d // 2], wb[:, d // 2:]], axis=-1)
    return lax.bitcast_convert_type(pairs, jnp.int32).reshape(e * _WORD_ROWS, 128)


@functools.lru_cache(maxsize=None)
def _peer_constants():
    lane_q = np.arange(8 * N_SEL) % 8
    expand = np.zeros((N_SEL, 8 * N_SEL), np.float32)
    expand[np.arange(8 * N_SEL) // 8, np.arange(8 * N_SEL)] = 1.0
    msk_dn = (lane_q[None, :] == np.arange(8)[:, None]).astype(np.float32)
    packed_of_chunk = np.argsort(np.array(_CHUNK_OF_PACKED))
    msk_up = (lane_q[None, :] == packed_of_chunk[:, None]).astype(np.float32)
    return expand, msk_dn, msk_up


_GROUP = 8
_N_WIN_BUF = 4


def _gather_rows(win_ref, buf, q, tbl_ref, g_ref):
    for k in range(N_SEL):
        row = pl.multiple_of(win_ref[buf, q, k], _WORD_ROWS)
        g_ref[_WORD_ROWS * k:_WORD_ROWS * (k + 1), :] = tbl_ref[pl.ds(row, _WORD_ROWS), :]


def _windowed_tokens(tt, idx_vmem, win_ref, sem, stage, compute, finish):
    n_win = tt // _GROUP
    trips = _N_WIN_BUF * _GROUP // 2

    def window_copy(win, buf):
        src = idx_vmem.at[jnp.minimum(win, n_win - 1)]
        return pltpu.make_async_copy(src, win_ref.at[buf], sem.at[buf])

    for b in range(_N_WIN_BUF):
        window_copy(b, b).start()
    window_copy(0, 0).wait()
    stage(0, 0, 0, 0)
    stage(1, 0, 1, 1)

    def body(i, _):
        w0 = _N_WIN_BUF * i
        base = _GROUP * w0
        for p in range(trips):
            t0 = base + 2 * p
            first_staged = 2 * p + 2
            if first_staged % _GROUP == 0:
                nw = first_staged // _GROUP
                window_copy(w0 + nw, nw % _N_WIN_BUF).wait()
                window_copy(w0 + nw - 1 + _N_WIN_BUF, (nw - 1) % _N_WIN_BUF).start()
            r0 = compute(t0, 0)
            r1 = compute(t0 + 1, 1)
            for slot in range(2):
                u = first_staged + slot
                stage(jnp.minimum(base + u, tt - 1), (u // _GROUP) % _N_WIN_BUF, u % _GROUP, slot)
            finish(t0, r0)
            finish(t0 + 1, r1)
        return 0

    lax.fori_loop(0, n_win // _N_WIN_BUF, body, 0)
    for b in range(1, _N_WIN_BUF):
        window_copy(n_win + b, b).wait()


def _peer_dn_kernel(idx_vmem, hm_ref, tbl_ref, msk_ref, et_ref, a_ref, win_ref, sem, y_ref, *g_refs, tt):
    def stage(t, buf, q, slot):
        _gather_rows(win_ref, buf, q, tbl_ref, g_refs[slot])

    def compute(t, slot):
        rows = pltpu.bitcast(g_refs[slot][...], BF16)
        return _dot_nt(hm_ref[t], rows)

    def finish(t, y):
        y_ref[t] = y[:8] + y[8:]

    _windowed_tokens(tt, idx_vmem, win_ref, sem, stage, compute, finish)
    s = jnp.sum(y_ref[...] * msk_ref[...], axis=1)
    a_ref[...] = _dot_split(s, et_ref[...])


def _peer_dn(idx, hm, tbl, tt):
    t = hm.shape[0]
    expand, msk_dn, _ = _peer_constants()
    consts = [jnp.asarray(msk_dn, F32), jnp.asarray(expand.T.copy(), BF16)]
    return pl.pallas_call(
        functools.partial(_peer_dn_kernel, tt=tt),
        grid=(t // tt,),
        in_specs=[pl.BlockSpec((tt // _GROUP, _GROUP, N_SEL), lambda i: (i, 0, 0)),
                  pl.BlockSpec((tt, 16, 128), lambda i: (i, 0, 0)),
                  pl.BlockSpec(memory_space=pltpu.VMEM)] + [_const_spec(a.shape) for a in consts],
        out_specs=pl.BlockSpec((tt, N_SEL), lambda i: (i, 0)),
        out_shape=jax.ShapeDtypeStruct((t, N_SEL), F32),
        scratch_shapes=[pltpu.SMEM((_N_WIN_BUF, _GROUP, N_SEL), jnp.int32), pltpu.SemaphoreType.DMA((_N_WIN_BUF,)),
                        pltpu.VMEM((tt, 8, 8 * N_SEL), F32)] + _gather_buffers(),
        compiler_params=_params("arbitrary"),
        name="peer_dn",
    )(idx, hm, tbl, *consts)


def _gate_kernel(a_ref, g_ref, w_ref):
    w_ref[...] = g_ref[...] * _gelu(a_ref[...])


def _gate(a, g):
    t = a.shape[0]
    blk = min(t, 2048)
    spec = pl.BlockSpec((blk, N_SEL), lambda i: (i, 0))
    return pl.pallas_call(
        _gate_kernel, grid=(t // blk,), in_specs=[spec, spec], out_specs=spec,
        out_shape=jax.ShapeDtypeStruct((t, N_SEL), F32),
        compiler_params=_params("arbitrary"), name="gate",
    )(a, g)


def _gather_buffers():
    return [pltpu.VMEM((_WORD_ROWS * N_SEL, 128), jnp.int32) for _ in range(2)]


def _peer_up_kernel(idx_vmem, w_ref, x1_ref, gate_ref, tbl_ref, e_ref, msk_ref, o_ref,
                    win_ref, sem, wexp_ref, lhs_ref, *g_refs, tt):
    wexp_ref[...] = _dot_split(w_ref[...], e_ref[...])
    gate = gate_ref[0]
    msk = msk_ref[...]
    top = lax.broadcasted_iota(jnp.int32, msk.shape, 0) < 8

    def stage(t, buf, q, slot):
        _gather_rows(win_ref, buf, q, tbl_ref, g_refs[slot])
        wsel = wexp_ref[pl.ds(t, 1), :] * msk
        hi = wsel.astype(BF16)
        lo = (wsel - hi.astype(F32)).astype(BF16)
        lhs_ref[slot] = jnp.where(top, hi, lo)

    def compute(t, slot):
        rows = pltpu.bitcast(g_refs[slot][...], BF16)
        return _dot(lhs_ref[slot], rows)

    def finish(t, o16):
        o_ref[t] = x1_ref[t] + gate * (o16[:8] + o16[8:])

    _windowed_tokens(tt, idx_vmem, win_ref, sem, stage, compute, finish)


def _peer_up(idx, wgt, x13, gate3, tbl, seq, tt):
    t = x13.shape[0]
    nsb = seq // tt
    expand, _, msk_up = _peer_constants()
    consts = [jnp.asarray(expand, BF16), jnp.asarray(np.concatenate([msk_up, msk_up], axis=0), F32)]
    row = pl.BlockSpec((tt, 8, 128), lambda i: (i, 0, 0))
    return pl.pallas_call(
        functools.partial(_peer_up_kernel, tt=tt),
        grid=(t // tt,),
        in_specs=[pl.BlockSpec((tt // _GROUP, _GROUP, N_SEL), lambda i: (i, 0, 0)),
                  pl.BlockSpec((tt, N_SEL), lambda i: (i, 0)),
                  row, pl.BlockSpec((1, 8, 128), lambda i: (i // nsb, 0, 0)),
                  pl.BlockSpec(memory_space=pltpu.VMEM)] + [_const_spec(a.shape) for a in consts],
        out_specs=row,
        out_shape=jax.ShapeDtypeStruct((t, 8, 128), F32),
        scratch_shapes=[pltpu.SMEM((_N_WIN_BUF, _GROUP, N_SEL), jnp.int32), pltpu.SemaphoreType.DMA((_N_WIN_BUF,)),
                        pltpu.VMEM((tt, 8 * N_SEL), F32), pltpu.VMEM((2, 16, 8 * N_SEL), BF16)]
        + _gather_buffers(),
        compiler_params=_params("arbitrary"),
        name="peer_up",
    )(idx, wgt, x13, gate3, tbl, *consts)


def _layer(x, c, positions, w_ada, b_ada, norm_mix, norm_ffn, w_in, lam_re, lam_im, log_dt, b_re, b_im,
           c_re, c_im, d_skip, w_glu, b_glu, q_a_norm, w_uq, kv_a_norm, w_ukv, q_norm, k_norm,
           out_norm_ssm, out_norm_attn, w_out, w_query, sub_keys, expert_down, expert_up):
    bsz, seq, _ = x.shape
    t = bsz * seq
    tm = min(256, seq)
    x2 = x.reshape(t, D_MODEL)

    mod = _ada(c, w_ada, b_ada)
    shift_m, scale_m, gate_m, shift_f, scale_f, gate_f = [
        m.reshape(bsz, 1, D_MODEL) for m in jnp.split(mod, N_ADA, axis=-1)]

    cs = _rope_tables(positions)
    u_tm, q, k, v = _inproj(x2, scale_m, shift_m, norm_mix, w_in, q_a_norm, w_uq, kv_a_norm, w_ukv,
                            q_norm, k_norm, cs, bsz, seq, tm)
    ya = _attention(q, k, v, bsz, seq, tm)

    ar, ai, bbr, bbi = _s5_params(lam_re, lam_im, log_dt, b_re, b_im)
    bd = jnp.concatenate([_block_diag(bbr), _block_diag(bbi)], axis=1).astype(BF16)
    cd = jnp.concatenate([_block_diag(jnp.transpose(c_re, (0, 2, 1))),
                          _block_diag(jnp.transpose(-c_im, (0, 2, 1)))], axis=0).astype(BF16)
    ys_tm = _s5(u_tm.reshape(seq * bsz, D_SSM), bd, ar.reshape(1, N_STATE), ai.reshape(1, N_STATE), cd,
                d_skip, w_glu, b_glu, out_norm_ssm, bsz, seq, min(32, seq))

    x1, hm, qp = _outproj(ys_tm.reshape(seq, bsz * D_SSM), ya, x2, gate_m, scale_f, shift_f,
                          out_norm_attn, w_out, norm_ffn, w_query, bsz, seq, tm)

    idx_t, g_t = _topk(qp, sub_keys, tm)
    idx = jnp.transpose(idx_t.reshape(N_SEL, t))
    g = jnp.transpose(g_t.reshape(N_SEL, t))
    idx_win = idx.reshape(t // _GROUP, _GROUP, N_SEL)

    tt = min(128, seq)
    a = _peer_dn(idx_win, hm.reshape(t, 16, 128), _pack_table(expert_down), tt)
    wgt = _gate(a, g)
    out = _peer_up(idx_win, wgt, x1.reshape(t, 8, 128), gate_f.reshape(bsz, 8, 128),
                   _pack_table(expert_up), seq, tt)
    return out.reshape(bsz, seq, D_MODEL)


def kernel(x, c, positions, w_ada, b_ada, norm_mix, norm_ffn, w_in, lam_re, lam_im, log_dt, b_re, b_im, c_re, c_im, d_skip, w_glu, b_glu, q_a_norm, w_uq, kv_a_norm, w_ukv, q_norm, k_norm, out_norm_ssm, out_norm_attn, w_out, w_query, sub_keys, expert_down, expert_up):
    for l in range(w_ada.shape[0]):
        x = _layer(x, c, positions, w_ada[l], b_ada[l], norm_mix[l], norm_ffn[l], w_in[l],
                   lam_re[l], lam_im[l], log_dt[l], b_re[l], b_im[l], c_re[l], c_im[l],
                   d_skip[l], w_glu[l], b_glu[l], q_a_norm[l], w_uq[l], kv_a_norm[l], w_ukv[l],
                   q_norm[l], k_norm[l], out_norm_ssm[l], out_norm_attn[l], w_out[l],
                   w_query[l], sub_keys[l], expert_down[l], expert_up[l])
    return x
```

```python
import functools
import math

import numpy as np
import jax
import jax.numpy as jnp
from jax import lax
from jax.experimental import pallas as pl
from jax.experimental.pallas import tpu as pltpu

F32 = jnp.float32
BF16 = jnp.bfloat16

D_MODEL = 1024
D_SSM = 512
SSM_GROUP = 16
N_SSM_GROUPS = 32
SSM_STATE = 64
N_STATE = N_SSM_GROUPS * SSM_STATE
N_HEADS = 8
QK_NOPE = 64
QK_ROPE = 32
QK_HEAD = 96
V_HEAD = 64
HEAD_PAD = 128
Q_LORA = 256
KV_LORA = 128
ROPE_THETA = 10000.0
PEER_HEADS = 8
PEER_KEYS = 128
PEER_TOPK = 16
PEER_HALF = 128
N_SEL = PEER_HEADS * PEER_TOPK
N_ADA = 6
EPS = 1e-6
GELU_C = math.sqrt(2.0 / math.pi)

VMEM_LIMIT = 48 * 1024 * 1024


def _dot(a, b):
    return jnp.dot(a, b, preferred_element_type=F32)


def _dot_nt(a, b):
    return lax.dot_general(a, b, (((1,), (1,)), ((), ())), preferred_element_type=F32)


def _dot_split(a, sel):
    hi = a.astype(BF16)
    lo = (a - hi.astype(F32)).astype(BF16)
    return _dot(hi, sel) + _dot(lo, sel)


def _rms(x):
    return x * lax.rsqrt(jnp.mean(x * x, axis=-1, keepdims=True) + EPS)


def _gelu(x):
    return 0.5 * x * (1.0 + jnp.tanh(GELU_C * (x + 0.044715 * x * x * x)))


def _params(*sem):
    return pltpu.CompilerParams(dimension_semantics=sem, vmem_limit_bytes=VMEM_LIMIT)


def _const_spec(shape):
    nd = len(shape)
    return pl.BlockSpec(shape, lambda *_: (0,) * nd)


def _ada_kernel(c_ref, w_ref, b_ref, o_ref):
    c = c_ref[...]
    s = c * jax.nn.sigmoid(c)
    o_ref[...] = _dot_split2(s, w_ref[...]) + b_ref[...]


def _dot_split2(a, w):
    ah = a.astype(BF16)
    al = (a - ah.astype(F32)).astype(BF16)
    wh = w.astype(BF16)
    wl = (w - wh.astype(F32)).astype(BF16)
    return _dot(ah, wh) + (_dot(ah, wl) + _dot(al, wh))


def _ada(c, w_ada, b_ada):
    bsz = c.shape[0]
    n = w_ada.shape[1]
    blk = D_MODEL
    return pl.pallas_call(
        _ada_kernel,
        grid=(n // blk,),
        in_specs=[_const_spec((bsz, D_MODEL)),
                  pl.BlockSpec((D_MODEL, blk), lambda j: (0, j)),
                  pl.BlockSpec((1, blk), lambda j: (0, j))],
        out_specs=pl.BlockSpec((bsz, blk), lambda j: (0, j)),
        out_shape=jax.ShapeDtypeStruct((bsz, n), F32),
        compiler_params=_params("arbitrary"),
        name="ada",
    )(c, w_ada, b_ada.reshape(1, n))


def _rope_kernel(pos_ref, freq_ref, cos_ref, sin_ref):
    ang = pos_ref[...].astype(F32) * freq_ref[...]
    cos_ref[...] = jnp.cos(ang)
    sin_ref[...] = jnp.sin(ang)


def _rope_tables(positions):
    half = QK_ROPE // 2
    t = positions.size
    rows = t * half // 128
    pos_rep = jnp.repeat(positions.reshape(-1), half).reshape(rows, 128)
    inv_freq = ROPE_THETA ** (-jnp.arange(half, dtype=F32) / half)
    freq_row = jnp.tile(inv_freq, 128 // half).reshape(1, 128)
    blk = min(rows, 512)
    cos_d, sin_d = pl.pallas_call(
        _rope_kernel,
        grid=(rows // blk,),
        in_specs=[pl.BlockSpec((blk, 128), lambda i: (i, 0)), _const_spec((1, 128))],
        out_specs=[pl.BlockSpec((blk, 128), lambda i: (i, 0))] * 2,
        out_shape=[jax.ShapeDtypeStruct((rows, 128), F32)] * 2,
        compiler_params=_params("arbitrary"),
        name="rope",
    )(pos_rep, freq_row)
    return jnp.concatenate(
        [cos_d.reshape(t, half), sin_d.reshape(t, half), jnp.zeros((t, 128 - 2 * half), F32)], axis=1)


def _s5par_kernel(lr_ref, li_ref, ldt_ref, bre_ref, bim_ref, ar_ref, ai_ref, bbr_ref, bbi_ref):
    lr = lr_ref[...]
    li = li_ref[...]
    dt = jnp.exp(ldt_ref[...])
    mag = jnp.exp(lr * dt)
    ar = mag * jnp.cos(li * dt)
    ai = mag * jnp.sin(li * dt)
    den = lr * lr + li * li
    nr = ar - 1.0
    ni = ai
    coef_r = (nr * lr + ni * li) / den
    coef_i = (ni * lr - nr * li) / den
    ar_ref[...] = ar
    ai_ref[...] = ai
    bre = bre_ref[...]
    bim = bim_ref[...]
    cr = coef_r[:, None, :]
    ci = coef_i[:, None, :]
    bbr_ref[...] = cr * bre - ci * bim
    bbi_ref[...] = cr * bim + ci * bre


def _s5_params(lam_re, lam_im, log_dt, b_re, b_im):
    g, p = lam_re.shape
    c = b_re.shape[-1]
    bre_t = jnp.transpose(b_re, (0, 2, 1))
    bim_t = jnp.transpose(b_im, (0, 2, 1))
    return pl.pallas_call(
        _s5par_kernel,
        out_shape=[jax.ShapeDtypeStruct((g, p), F32), jax.ShapeDtypeStruct((g, p), F32),
                   jax.ShapeDtypeStruct((g, c, p), F32), jax.ShapeDtypeStruct((g, c, p), F32)],
        name="s5par",
    )(lam_re, lam_im, log_dt.reshape(g, 1), bre_t, bim_t)


def _block_diag(blocks):
    g, r, c = blocks.shape
    eye = jnp.eye(g, dtype=blocks.dtype)
    return (blocks[:, :, None, :] * eye[:, None, :, None]).reshape(g * r, g * c)


@functools.lru_cache(maxsize=None)
def _layout_constants():
    hp, nh = HEAD_PAD, N_HEADS
    width = nh * hp
    half = QK_ROPE // 2
    place = np.zeros((128, width), np.float32)
    hsel = np.zeros((width, 128), np.float32)
    rot = np.zeros((1, width), np.float32)
    ecos = np.zeros((128, width), np.float32)
    esin = np.zeros((128, width), np.float32)
    for h in range(nh):
        base = h * hp
        hsel[base:base + QK_HEAD, h] = 1.0
        for j in range(QK_ROPE):
            place[j, base + QK_NOPE + j] = 1.0
        for j in range(half):
            c1 = base + QK_NOPE + j
            c2 = c1 + half
            rot[0, c1] = 1.0
            ecos[j, c1] = 1.0
            ecos[j, c2] = 1.0
            esin[half + j, c1] = 1.0
            esin[half + j, c2] = 1.0
    ones_nope = (ecos.sum(axis=0, keepdims=True) == 0).astype(np.float32)
    return place, hsel, rot, ecos, esin, ones_nope


def _pad_heads(w, head_dim):
    k = w.shape[0]
    w = w.reshape(k, N_HEADS, head_dim)
    w = jnp.pad(w, ((0, 0), (0, 0), (0, HEAD_PAD - head_dim)))
    return w.reshape(k, N_HEADS * HEAD_PAD)


def _inproj_kernel(x_ref, scale_ref, shift_ref, nmix_ref, win_ref, qan_ref, wuq_ref, kvan_ref,
                   wk_ref, wv_ref, place_ref, hsel_ref, hselt_ref, qg_ref, kg_ref, rot_ref,
                   ecos_ref, esin_ref, nope_ref, cs_ref,
                   u_ref, q_ref, k_ref, v_ref):
    x = x_ref[...]
    h = _rms(x) * nmix_ref[...]
    h = h * (1.0 + scale_ref[0]) + shift_ref[0]
    proj = _dot(h.astype(BF16), win_ref[...])
    u_ref[...] = proj[:, :D_SSM].astype(u_ref.dtype)
    o1 = D_SSM + Q_LORA
    o2 = o1 + KV_LORA
    cq = _rms(proj[:, D_SSM:o1]) * qan_ref[...]
    ckv = (_rms(proj[:, o1:o2]) * kvan_ref[...]).astype(BF16)
    kr = proj[:, o2:]
    q = _dot(cq.astype(BF16), wuq_ref[...])
    k = _dot(ckv, wk_ref[...]) + _dot_split(kr, place_ref[...])
    v_ref[...] = _dot(ckv, wv_ref[...]).astype(v_ref.dtype)

    cs = cs_ref[...]
    cos = _dot_split(cs, ecos_ref[...]) + nope_ref[...]
    sin = _dot_split(cs, esin_ref[...])

    def head_norm_rope(z, gain):
        ssq = _dot_split(z * z, hsel_ref[...])
        r = lax.rsqrt(ssq * (1.0 / QK_HEAD) + EPS)
        zn = z * _dot_split(r, hselt_ref[...]) * gain
        width = zn.shape[1]
        half = QK_ROPE // 2
        rotated = jnp.where(rot_ref[...] > 0.5, -pltpu.roll(zn, width - half, axis=1),
                            pltpu.roll(zn, half, axis=1))
        return zn * cos + rotated * sin

    qr = head_norm_rope(q, qg_ref[...]) * (QK_HEAD ** -0.5)
    q_ref[...] = qr.astype(q_ref.dtype)
    k_ref[...] = head_norm_rope(k, kg_ref[...]).astype(k_ref.dtype)


def _inproj(x2, scale_m, shift_m, norm_mix, w_in, q_a_norm, w_uq, kv_a_norm, w_ukv, q_norm, k_norm,
            cs, bsz, seq, tm):
    t = bsz * seq
    nsb = seq // tm
    place, hsel, rot, ecos, esin, ones_nope = _layout_constants()
    width = N_HEADS * HEAD_PAD
    win = jnp.pad(w_in, ((0, 0), (0, D_MODEL - w_in.shape[1]))).astype(BF16)
    wuq = _pad_heads(w_uq, QK_HEAD).astype(BF16)
    wkv = w_ukv.reshape(KV_LORA, N_HEADS, QK_NOPE + V_HEAD)
    wk = _pad_heads(wkv[:, :, :QK_NOPE].reshape(KV_LORA, N_HEADS * QK_NOPE), QK_NOPE).astype(BF16)
    wv = wkv[:, :, QK_NOPE:].reshape(KV_LORA, N_HEADS * V_HEAD).astype(BF16)
    qg = _pad_heads(jnp.tile(q_norm, N_HEADS).reshape(1, -1), QK_HEAD)
    kg = _pad_heads(jnp.tile(k_norm, N_HEADS).reshape(1, -1), QK_HEAD)
    consts = [
        norm_mix.reshape(1, D_MODEL), win, q_a_norm.reshape(1, Q_LORA), wuq,
        kv_a_norm.reshape(1, KV_LORA), wk, wv,
        jnp.asarray(place, BF16), jnp.asarray(hsel, BF16), jnp.asarray(hsel.T.copy(), BF16), qg, kg,
        jnp.asarray(rot, F32), jnp.asarray(ecos, BF16), jnp.asarray(esin, BF16),
        jnp.asarray(ones_nope, F32),
    ]
    mod_spec = pl.BlockSpec((1, 1, D_MODEL), lambda i: (i // nsb, 0, 0))
    in_specs = ([pl.BlockSpec((tm, D_MODEL), lambda i: (i, 0)), mod_spec, mod_spec]
                + [_const_spec(a.shape) for a in consts]
                + [pl.BlockSpec((tm, 128), lambda i: (i, 0))])
    out_specs = [
        pl.BlockSpec((tm, D_SSM), lambda i: (i % nsb, i // nsb)),
        pl.BlockSpec((tm, width), lambda i: (i, 0)),
        pl.BlockSpec((tm, width), lambda i: (i, 0)),
        pl.BlockSpec((tm, N_HEADS * V_HEAD), lambda i: (i, 0)),
    ]
    out_shape = [
        jax.ShapeDtypeStruct((seq, bsz * D_SSM), BF16),
        jax.ShapeDtypeStruct((t, width), BF16),
        jax.ShapeDtypeStruct((t, width), BF16),
        jax.ShapeDtypeStruct((t, N_HEADS * V_HEAD), BF16),
    ]
    return pl.pallas_call(
        _inproj_kernel, grid=(t // tm,), in_specs=in_specs, out_specs=out_specs, out_shape=out_shape,
        compiler_params=_params("arbitrary"), name="inproj",
    )(x2, scale_m, shift_m, *consts, cs)


def _attn_kernel(q_ref, k_ref, v_ref, o_ref, s_ref, *, tq):
    qi = pl.program_id(2)
    lane = lax.broadcasted_iota(jnp.int32, (1, 2 * V_HEAD), 1)
    first = lane < V_HEAD

    def scores(j, slot):
        kb = k_ref[pl.ds(pl.multiple_of(j * tq, tq), tq), :]
        s_ref[slot, 0] = _dot_nt(q_ref[:, :HEAD_PAD], kb[:, :HEAD_PAD])
        s_ref[slot, 1] = _dot_nt(q_ref[:, HEAD_PAD:], kb[:, HEAD_PAD:])

    def update(j, slot, carry, masked):
        m0, m1, acc0, acc1 = carry
        vb = v_ref[pl.ds(pl.multiple_of(j * tq, tq), tq), :]
        one = jnp.ones_like(vb)
        s0 = s_ref[slot, 0]
        s1 = s_ref[slot, 1]
        if masked:
            row = lax.broadcasted_iota(jnp.int32, (tq, tq), 0)
            col = lax.broadcasted_iota(jnp.int32, (tq, tq), 1)
            keep = col <= row
            s0 = jnp.where(keep, s0, -jnp.inf)
            s1 = jnp.where(keep, s1, -jnp.inf)
        n0 = jnp.maximum(m0, jnp.max(s0, axis=-1, keepdims=True))
        n1 = jnp.maximum(m1, jnp.max(s1, axis=-1, keepdims=True))
        p0 = jnp.exp(s0 - n0).astype(BF16)
        p1 = jnp.exp(s1 - n1).astype(BF16)
        acc0 = jnp.exp(m0 - n0) * acc0 + _dot(p0, jnp.where(first, vb, one))
        acc1 = jnp.exp(m1 - n1) * acc1 + _dot(p1, jnp.where(first, one, vb))
        return n0, n1, acc0, acc1

    def body(j, carry):
        slot = j % 2
        carry = update(j, slot, carry, False)
        scores(j + 1, 1 - slot)
        return carry

    scores(0, 0)
    neg = jnp.full((tq, 1), -jnp.inf, F32)
    zacc = jnp.zeros((tq, 2 * V_HEAD), F32)
    carry = lax.fori_loop(0, qi, body, (neg, neg, zacc, zacc))
    _, _, acc0, acc1 = update(qi, qi % 2, carry, True)
    out0 = acc0 * pltpu.roll(1.0 / acc0, V_HEAD, axis=1)
    out1 = acc1 * pltpu.roll(1.0 / acc1, V_HEAD, axis=1)
    o_ref[...] = jnp.where(first, out0, out1).astype(o_ref.dtype)


def _attention(q, k, v, bsz, seq, tq):
    nq = seq // tq
    t = bsz * seq
    return pl.pallas_call(
        functools.partial(_attn_kernel, tq=tq),
        grid=(bsz, N_HEADS // 2, nq),
        in_specs=[pl.BlockSpec((tq, 2 * HEAD_PAD), lambda b, h, i: (b * nq + i, h)),
                  pl.BlockSpec((seq, 2 * HEAD_PAD), lambda b, h, i: (b, h)),
                  pl.BlockSpec((seq, 2 * V_HEAD), lambda b, h, i: (b, h))],
        out_specs=pl.BlockSpec((tq, 2 * V_HEAD), lambda b, h, i: (b * nq + i, h)),
        out_shape=jax.ShapeDtypeStruct((t, N_HEADS * V_HEAD), BF16),
        scratch_shapes=[pltpu.VMEM((2, 2, tq, tq), F32)],
        compiler_params=_params("arbitrary", "arbitrary", "arbitrary"),
        name="attn",
    )(q, k, v)


def _s5_kernel(u_ref, bd_ref, ar_ref, ai_ref, cd_ref, dsk_ref, wglu_ref, bglu_ref, gain_ref,
               y_ref, bu_ref, st_ref, *, lc, bsz, cb):
    @pl.when(pl.program_id(0) == 0)
    def _():
        st_ref[...] = jnp.zeros_like(st_ref)

    u = u_ref[...]
    bu_ref[...] = _dot(u, bd_ref[...])

    for c0 in range(0, N_STATE, cb):
        ar = ar_ref[:, c0:c0 + cb]
        ai = ai_ref[:, c0:c0 + cb]

        def step(t, carry, c0=c0, ar=ar, ai=ai):
            xr, xi = carry
            r0 = pl.multiple_of(t * bsz, bsz)
            bur = bu_ref[pl.ds(r0, bsz), c0:c0 + cb]
            bui = bu_ref[pl.ds(r0, bsz), N_STATE + c0:N_STATE + c0 + cb]
            nxr = ar * xr - ai * xi + bur
            nxi = ar * xi + ai * xr + bui
            bu_ref[pl.ds(r0, bsz), c0:c0 + cb] = nxr
            bu_ref[pl.ds(r0, bsz), N_STATE + c0:N_STATE + c0 + cb] = nxi
            return nxr, nxi

        xr0 = st_ref[:, c0:c0 + cb]
        xi0 = st_ref[:, N_STATE + c0:N_STATE + c0 + cb]
        xr, xi = lax.fori_loop(0, lc, step, (xr0, xi0))
        st_ref[:, c0:c0 + cb] = xr
        st_ref[:, N_STATE + c0:N_STATE + c0 + cb] = xi

    y = _dot(bu_ref[...].astype(BF16), cd_ref[...]) + dsk_ref[...] * u.astype(F32)
    y = _gelu(y)
    y = y * jax.nn.sigmoid(_dot(y.astype(BF16), wglu_ref[...]) + bglu_ref[...])
    y_ref[...] = (_rms(y) * gain_ref[...]).astype(y_ref.dtype)


def _s5(u_tm, bd, ar, ai, cd, d_skip, w_glu, b_glu, gain, bsz, seq, lc):
    rows = lc * bsz
    consts = [bd, ar, ai, cd, d_skip.reshape(1, D_SSM), w_glu.astype(BF16), b_glu.reshape(1, D_SSM),
              gain.reshape(1, D_SSM)]
    return pl.pallas_call(
        functools.partial(_s5_kernel, lc=lc, bsz=bsz, cb=512),
        grid=(seq // lc,),
        in_specs=[pl.BlockSpec((rows, D_SSM), lambda j: (j, 0))] + [_const_spec(a.shape) for a in consts],
        out_specs=pl.BlockSpec((rows, D_SSM), lambda j: (j, 0)),
        out_shape=jax.ShapeDtypeStruct((seq * bsz, D_SSM), BF16),
        scratch_shapes=[pltpu.VMEM((rows, 2 * N_STATE), F32), pltpu.VMEM((bsz, 2 * N_STATE), F32)],
        compiler_params=_params("arbitrary"),
        name="s5",
    )(u_tm, *consts)


def _outproj_kernel(ys_ref, ya_ref, x_ref, gate_ref, scale_ref, shift_ref, ga_ref, wos_ref, woa_ref,
                    nffn_ref, wq_ref, x1_ref, h2_ref, qp_ref):
    ya = _rms(ya_ref[...].astype(F32)) * ga_ref[...]
    y = _dot(ys_ref[...], wos_ref[...]) + _dot(ya.astype(BF16), woa_ref[...])
    x1 = x_ref[...] + gate_ref[0] * y
    x1_ref[...] = x1
    h2 = _rms(x1) * nffn_ref[...]
    h2 = h2 * (1.0 + scale_ref[0]) + shift_ref[0]
    hi = h2.astype(BF16)
    lo = (h2 - hi.astype(F32)).astype(BF16)
    h2_ref[...] = jnp.concatenate(
        [part[:, c * 128:(c + 1) * 128] for part in (hi, lo) for c in _CHUNK_OF_PACKED], axis=1)
    qp = _dot(hi, wq_ref[...]).astype(qp_ref.dtype)
    for h in range(PEER_HEADS):
        qp_ref[h] = qp[:, h * 2 * PEER_HALF:(h + 1) * 2 * PEER_HALF]


def _outproj(ys_tm, ya, x2, gate_m, scale_f, shift_f, out_norm_attn, w_out, norm_ffn, w_query,
             bsz, seq, tm):
    t = bsz * seq
    nsb = seq // tm
    wo = w_out.astype(BF16)
    consts = [out_norm_attn.reshape(1, -1), wo[:D_SSM], wo[D_SSM:], norm_ffn.reshape(1, D_MODEL),
              w_query.astype(BF16)]
    mod_spec = pl.BlockSpec((1, 1, D_MODEL), lambda i: (i // nsb, 0, 0))
    row_spec = pl.BlockSpec((tm, D_MODEL), lambda i: (i, 0))
    return pl.pallas_call(
        _outproj_kernel,
        grid=(t // tm,),
        in_specs=[pl.BlockSpec((tm, D_SSM), lambda i: (i % nsb, i // nsb)),
                  pl.BlockSpec((tm, D_SSM), lambda i: (i, 0)),
                  row_spec, mod_spec, mod_spec, mod_spec] + [_const_spec(a.shape) for a in consts],
        out_specs=[row_spec, pl.BlockSpec((tm, 2 * D_MODEL), lambda i: (i, 0)),
                   pl.BlockSpec((PEER_HEADS, tm, 2 * PEER_HALF), lambda i: (0, i, 0))],
        out_shape=[jax.ShapeDtypeStruct((t, D_MODEL), F32), jax.ShapeDtypeStruct((t, 2 * D_MODEL), BF16),
                   jax.ShapeDtypeStruct((PEER_HEADS, t, 2 * PEER_HALF), BF16)],
        compiler_params=_params("arbitrary"),
        name="outproj",
    )(ys_tm, ya, x2, gate_m, scale_f, shift_f, *consts)


def _topk_rows(s, k, payload=None):
    n_rows = s.shape[0]
    iota = lax.broadcasted_iota(jnp.int32, s.shape, 0)
    vals, picks = [], []
    for _ in range(k):
        m = jnp.max(s, axis=0, keepdims=True)
        ix = jnp.min(jnp.where(s == m, iota, n_rows), axis=0, keepdims=True)
        hit = iota == ix
        vals.append(m)
        if payload is None:
            picks.append(ix)
        else:
            picks.append(jnp.sum(jnp.where(hit, payload, 0.0), axis=0, keepdims=True))
        s = jnp.where(hit, -jnp.inf, s)
    return jnp.concatenate(vals, axis=0), jnp.concatenate(picks, axis=0)


@functools.lru_cache(maxsize=None)
def _candidate_constants():
    k = PEER_TOPK
    pairs = [(a, b) for a in range(k) for b in range(k) if (a + 1) * (b + 1) <= k]
    rows = -(-len(pairs) // 16) * 16
    sel_a = np.zeros((rows, 128), np.float32)
    sel_b = np.zeros((rows, 128), np.float32)
    pad = np.zeros((rows, 1), np.float32)
    for r, (a, b) in enumerate(pairs):
        sel_a[r, a] = 1.0
        sel_b[r, b] = 1.0
    pad[len(pairs):] = -np.inf
    return sel_a, sel_b, pad


def _select_rows(sel, a):
    hi = a.astype(BF16)
    r1 = a - hi.astype(F32)
    mid = r1.astype(BF16)
    lo = (r1 - mid.astype(F32)).astype(BF16)
    return (_dot(sel, hi) + _dot(sel, mid)) + _dot(sel, lo)


def _topk_kernel(q_ref, keys_ref, sela_ref, selb_ref, pad_ref, idx_ref, g_ref):
    k = PEER_TOPK
    tm = q_ref.shape[1]
    fill = jnp.zeros((128 - k, tm), F32)

    def head(h, _):
        q = q_ref[h]
        s1 = _dot_nt(keys_ref[0], q[:, :PEER_HALF])
        s2 = _dot_nt(keys_ref[1], q[:, PEER_HALF:])
        v1, i1 = _topk_rows(s1, k)
        v2, i2 = _topk_rows(s2, k)
        sela = sela_ref[...]
        selb = selb_ref[...]
        cand = (_select_rows(sela, jnp.concatenate([v1, fill], axis=0))
                + _select_rows(selb, jnp.concatenate([v2, fill], axis=0))) + pad_ref[...]
        e1 = _dot(sela, jnp.concatenate([i1.astype(F32), fill], axis=0).astype(BF16))
        e2 = _dot(selb, jnp.concatenate([i2.astype(F32), fill], axis=0).astype(BF16))
        top_s, top_e = _topk_rows(cand, k, payload=e1 * PEER_KEYS + e2)
        idx_ref[h] = top_e.astype(jnp.int32) * _WORD_ROWS
        e = jnp.exp(top_s - top_s[0:1, :])
        g_ref[h] = e / jnp.sum(e, axis=0, keepdims=True)
        return 0

    lax.fori_loop(0, PEER_HEADS, head, 0)


def _topk(qp, sub_keys, tm):
    t = qp.shape[1]
    sel_a, sel_b, pad = _candidate_constants()
    consts = [sub_keys.astype(BF16), jnp.asarray(sel_a, BF16), jnp.asarray(sel_b, BF16), jnp.asarray(pad, F32)]
    return pl.pallas_call(
        _topk_kernel,
        grid=(t // tm,),
        in_specs=[pl.BlockSpec((PEER_HEADS, tm, 2 * PEER_HALF), lambda i: (0, i, 0))]
        + [_const_spec(a.shape) for a in consts],
        out_specs=[pl.BlockSpec((PEER_HEADS, PEER_TOPK, tm), lambda i: (0, 0, i))] * 2,
        out_shape=[jax.ShapeDtypeStruct((PEER_HEADS, PEER_TOPK, t), jnp.int32),
                   jax.ShapeDtypeStruct((PEER_HEADS, PEER_TOPK, t), F32)],
        compiler_params=_params("arbitrary"),
        name="topk",
    )(qp, *consts)


_CHUNK_OF_PACKED = tuple((q % 2) * 4 + q // 2 for q in range(8))
_WORD_ROWS = D_MODEL // 256


def _pack_table(w):
    e, d = w.shape
    wb = w.astype(BF16)
    pairs = jnp.stack([wb[:, :d // 2], wb[:, d // 2:]], axis=-1)
    return lax.bitcast_convert_type(pairs, jnp.int32).reshape(e * _WORD_ROWS, 128)


@functools.lru_cache(maxsize=None)
def _peer_constants():
    lane_q = np.arange(8 * N_SEL) % 8
    expand = np.zeros((N_SEL, 8 * N_SEL), np.float32)
    expand[np.arange(8 * N_SEL) // 8, np.arange(8 * N_SEL)] = 1.0
    msk_dn = (lane_q[None, :] == np.arange(8)[:, None]).astype(np.float32)
    packed_of_chunk = np.argsort(np.array(_CHUNK_OF_PACKED))
    msk_up = (lane_q[None, :] == packed_of_chunk[:, None]).astype(np.float32)
    return expand, msk_dn, msk_up


_GROUP = 8
_N_WIN_BUF = 4


def _gather_rows(win_ref, buf, q, tbl_ref, g_ref):
    for k in range(N_SEL):
        row = pl.multiple_of(win_ref[buf, q, k], _WORD_ROWS)
        g_ref[_WORD_ROWS * k:_WORD_ROWS * (k + 1), :] = tbl_ref[pl.ds(row, _WORD_ROWS), :]


def _windowed_tokens(tt, idx_vmem, win_ref, sem, stage, compute, finish):
    n_win = tt // _GROUP
    trips = _N_WIN_BUF * _GROUP // 2

    def window_copy(win, buf):
        src = idx_vmem.at[jnp.minimum(win, n_win - 1)]
        return pltpu.make_async_copy(src, win_ref.at[buf], sem.at[buf])

    for b in range(_N_WIN_BUF):
        window_copy(b, b).start()
    window_copy(0, 0).wait()
    stage(0, 0, 0, 0)
    stage(1, 0, 1, 1)

    def body(i, _):
        w0 = _N_WIN_BUF * i
        base = _GROUP * w0
        for p in range(trips):
            t0 = base + 2 * p
            first_staged = 2 * p + 2
            if first_staged % _GROUP == 0:
                nw = first_staged // _GROUP
                window_copy(w0 + nw, nw % _N_WIN_BUF).wait()
                window_copy(w0 + nw - 1 + _N_WIN_BUF, (nw - 1) % _N_WIN_BUF).start()
            r0 = compute(t0, 0)
            r1 = compute(t0 + 1, 1)
            for slot in range(2):
                u = first_staged + slot
                stage(jnp.minimum(base + u, tt - 1), (u // _GROUP) % _N_WIN_BUF, u % _GROUP, slot)
            finish(t0, r0)
            finish(t0 + 1, r1)
        return 0

    lax.fori_loop(0, n_win // _N_WIN_BUF, body, 0)
    for b in range(1, _N_WIN_BUF):
        window_copy(n_win + b, b).wait()


def _peer_dn_kernel(idx_vmem, hm_ref, tbl_ref, msk_ref, et_ref, a_ref, win_ref, sem, y_ref, *g_refs, tt):
    def stage(t, buf, q, slot):
        _gather_rows(win_ref, buf, q, tbl_ref, g_refs[slot])

    def compute(t, slot):
        rows = pltpu.bitcast(g_refs[slot][...], BF16)
        return _dot_nt(hm_ref[t], rows)

    def finish(t, y):
        y_ref[t] = y[:8] + y[8:]

    _windowed_tokens(tt, idx_vmem, win_ref, sem, stage, compute, finish)
    s = jnp.sum(y_ref[...] * msk_ref[...], axis=1)
    a_ref[...] = _dot_split(s, et_ref[...])


def _peer_dn(idx, hm, tbl, tt):
    t = hm.shape[0]
    expand, msk_dn, _ = _peer_constants()
    consts = [jnp.asarray(msk_dn, F32), jnp.asarray(expand.T.copy(), BF16)]
    return pl.pallas_call(
        functools.partial(_peer_dn_kernel, tt=tt),
        grid=(t // tt,),
        in_specs=[pl.BlockSpec((tt // _GROUP, _GROUP, N_SEL), lambda i: (i, 0, 0)),
                  pl.BlockSpec((tt, 16, 128), lambda i: (i, 0, 0)),
                  pl.BlockSpec(memory_space=pltpu.VMEM)] + [_const_spec(a.shape) for a in consts],
        out_specs=pl.BlockSpec((tt, N_SEL), lambda i: (i, 0)),
        out_shape=jax.ShapeDtypeStruct((t, N_SEL), F32),
        scratch_shapes=[pltpu.SMEM((_N_WIN_BUF, _GROUP, N_SEL), jnp.int32), pltpu.SemaphoreType.DMA((_N_WIN_BUF,)),
                        pltpu.VMEM((tt, 8, 8 * N_SEL), F32)] + _gather_buffers(),
        compiler_params=_params("arbitrary"),
        name="peer_dn",
    )(idx, hm, tbl, *consts)


def _gate_kernel(a_ref, g_ref, w_ref):
    w_ref[...] = g_ref[...] * _gelu(a_ref[...])


def _gate(a, g):
    t = a.shape[0]
    blk = min(t, 2048)
    spec = pl.BlockSpec((blk, N_SEL), lambda i: (i, 0))
    return pl.pallas_call(
        _gate_kernel, grid=(t // blk,), in_specs=[spec, spec], out_specs=spec,
        out_shape=jax.ShapeDtypeStruct((t, N_SEL), F32),
        compiler_params=_params("arbitrary"), name="gate",
    )(a, g)


def _gather_buffers():
    return [pltpu.VMEM((_WORD_ROWS * N_SEL, 128), jnp.int32) for _ in range(2)]


def _peer_up_kernel(idx_vmem, w_ref, x1_ref, gate_ref, tbl_ref, e_ref, msk_ref, o_ref,
                    win_ref, sem, wexp_ref, lhs_ref, *g_refs, tt):
    wexp_ref[...] = _dot_split(w_ref[...], e_ref[...])
    gate = gate_ref[0]
    msk = msk_ref[...]
    top = lax.broadcasted_iota(jnp.int32, msk.shape, 0) < 8

    def stage(t, buf, q, slot):
        _gather_rows(win_ref, buf, q, tbl_ref, g_refs[slot])
        wsel = wexp_ref[pl.ds(t, 1), :] * msk
        hi = wsel.astype(BF16)
        lo = (wsel - hi.astype(F32)).astype(BF16)
        lhs_ref[slot] = jnp.where(top, hi, lo)

    def compute(t, slot):
        rows = pltpu.bitcast(g_refs[slot][...], BF16)
        return _dot(lhs_ref[slot], rows)

    def finish(t, o16):
        o_ref[t] = x1_ref[t] + gate * (o16[:8] + o16[8:])

    _windowed_tokens(tt, idx_vmem, win_ref, sem, stage, compute, finish)


def _peer_up(idx, wgt, x13, gate3, tbl, seq, tt):
    t = x13.shape[0]
    nsb = seq // tt
    expand, _, msk_up = _peer_constants()
    consts = [jnp.asarray(expand, BF16), jnp.asarray(np.concatenate([msk_up, msk_up], axis=0), F32)]
    row = pl.BlockSpec((tt, 8, 128), lambda i: (i, 0, 0))
    return pl.pallas_call(
        functools.partial(_peer_up_kernel, tt=tt),
        grid=(t // tt,),
        in_specs=[pl.BlockSpec((tt // _GROUP, _GROUP, N_SEL), lambda i: (i, 0, 0)),
                  pl.BlockSpec((tt, N_SEL), lambda i: (i, 0)),
                  row, pl.BlockSpec((1, 8, 128), lambda i: (i // nsb, 0, 0)),
                  pl.BlockSpec(memory_space=pltpu.VMEM)] + [_const_spec(a.shape) for a in consts],
        out_specs=row,
        out_shape=jax.ShapeDtypeStruct((t, 8, 128), F32),
        scratch_shapes=[pltpu.SMEM((_N_WIN_BUF, _GROUP, N_SEL), jnp.int32), pltpu.SemaphoreType.DMA((_N_WIN_BUF,)),
                        pltpu.VMEM((tt, 8 * N_SEL), F32), pltpu.VMEM((2, 16, 8 * N_SEL), BF16)]
        + _gather_buffers(),
        compiler_params=_params("arbitrary"),
        name="peer_up",
    )(idx, wgt, x13, gate3, tbl, *consts)


def _layer(x, c, positions, w_ada, b_ada, norm_mix, norm_ffn, w_in, lam_re, lam_im, log_dt, b_re, b_im,
           c_re, c_im, d_skip, w_glu, b_glu, q_a_norm, w_uq, kv_a_norm, w_ukv, q_norm, k_norm,
           out_norm_ssm, out_norm_attn, w_out, w_query, sub_keys, expert_down, expert_up):
    bsz, seq, _ = x.shape
    t = bsz * seq
    tm = min(256, seq)
    x2 = x.reshape(t, D_MODEL)

    mod = _ada(c, w_ada, b_ada)
    shift_m, scale_m, gate_m, shift_f, scale_f, gate_f = [
        m.reshape(bsz, 1, D_MODEL) for m in jnp.split(mod, N_ADA, axis=-1)]

    cs = _rope_tables(positions)
    u_tm, q, k, v = _inproj(x2, scale_m, shift_m, norm_mix, w_in, q_a_norm, w_uq, kv_a_norm, w_ukv,
                            q_norm, k_norm, cs, bsz, seq, tm)
    ya = _attention(q, k, v, bsz, seq, tm)

    ar, ai, bbr, bbi = _s5_params(lam_re, lam_im, log_dt, b_re, b_im)
    bd = jnp.concatenate([_block_diag(bbr), _block_diag(bbi)], axis=1).astype(BF16)
    cd = jnp.concatenate([_block_diag(jnp.transpose(c_re, (0, 2, 1))),
                          _block_diag(jnp.transpose(-c_im, (0, 2, 1)))], axis=0).astype(BF16)
    ys_tm = _s5(u_tm.reshape(seq * bsz, D_SSM), bd, ar.reshape(1, N_STATE), ai.reshape(1, N_STATE), cd,
                d_skip, w_glu, b_glu, out_norm_ssm, bsz, seq, min(32, seq))

    x1, hm, qp = _outproj(ys_tm.reshape(seq, bsz * D_SSM), ya, x2, gate_m, scale_f, shift_f,
                          out_norm_attn, w_out, norm_ffn, w_query, bsz, seq, tm)

    idx_t, g_t = _topk(qp, sub_keys, tm)
    idx = jnp.transpose(idx_t.reshape(N_SEL, t))
    g = jnp.transpose(g_t.reshape(N_SEL, t))
    idx_win = idx.reshape(t // _GROUP, _GROUP, N_SEL)

    tt = min(128, seq)
    a = _peer_dn(idx_win, hm.reshape(t, 16, 128), _pack_table(expert_down), tt)
    wgt = _gate(a, g)
    out = _peer_up(idx_win, wgt, x1.reshape(t, 8, 128), gate_f.reshape(bsz, 8, 128),
                   _pack_table(expert_up), seq, tt)
    return out.reshape(bsz, seq, D_MODEL)


def kernel(x, c, positions, w_ada, b_ada, norm_mix, norm_ffn, w_in, lam_re, lam_im, log_dt, b_re, b_im, c_re, c_im, d_skip, w_glu, b_glu, q_a_norm, w_uq, kv_a_norm, w_ukv, q_norm, k_norm, out_norm_ssm, out_norm_attn, w_out, w_query, sub_keys, expert_down, expert_up):
    for l in range(w_ada.shape[0]):
        x = _layer(x, c, positions, w_ada[l], b_ada[l], norm_mix[l], norm_ffn[l], w_in[l],
                   lam_re[l], lam_im[l], log_dt[l], b_re[l], b_im[l], c_re[l], c_im[l],
                   d_skip[l], w_glu[l], b_glu[l], q_a_norm[l], w_uq[l], kv_a_norm[l], w_ukv[l],
                   q_norm[l], k_norm[l], out_norm_ssm[l], out_norm_attn[l], w_out[l],
                   w_query[l], sub_keys[l], expert_down[l], expert_up[l])
    return x
```

```python
import functools
import math

import numpy as np
import jax
import jax.numpy as jnp
from jax import lax
from jax.experimental import pallas as pl
from jax.experimental.pallas import tpu as pltpu

F32 = jnp.float32
BF16 = jnp.bfloat16

D_MODEL = 1024
D_SSM = 512
SSM_GROUP = 16
N_SSM_GROUPS = 32
SSM_STATE = 64
N_STATE = N_SSM_GROUPS * SSM_STATE
N_HEADS = 8
QK_NOPE = 64
QK_ROPE = 32
QK_HEAD = 96
V_HEAD = 64
HEAD_PAD = 128
Q_LORA = 256
KV_LORA = 128
ROPE_THETA = 10000.0
PEER_HEADS = 8
PEER_KEYS = 128
PEER_TOPK = 16
PEER_HALF = 128
N_SEL = PEER_HEADS * PEER_TOPK
N_ADA = 6
EPS = 1e-6
GELU_C = math.sqrt(2.0 / math.pi)

VMEM_LIMIT = 48 * 1024 * 1024


def _dot(a, b):
    return jnp.dot(a, b, preferred_element_type=F32)


def _dot_nt(a, b):
    return lax.dot_general(a, b, (((1,), (1,)), ((), ())), preferred_element_type=F32)


def _dot_split(a, sel):
    hi = a.astype(BF16)
    lo = (a - hi.astype(F32)).astype(BF16)
    return _dot(hi, sel) + _dot(lo, sel)


def _rms(x):
    return x * lax.rsqrt(jnp.mean(x * x, axis=-1, keepdims=True) + EPS)


def _gelu(x):
    return 0.5 * x * (1.0 + jnp.tanh(GELU_C * (x + 0.044715 * x * x * x)))


def _params(*sem):
    return pltpu.CompilerParams(dimension_semantics=sem, vmem_limit_bytes=VMEM_LIMIT)


def _const_spec(shape):
    nd = len(shape)
    return pl.BlockSpec(shape, lambda *_: (0,) * nd)


def _ada_kernel(c_ref, w_ref, b_ref, o_ref):
    c = c_ref[...]
    s = c * jax.nn.sigmoid(c)
    o_ref[...] = _dot_split2(s, w_ref[...]) + b_ref[...]


def _dot_split2(a, w):
    ah = a.astype(BF16)
    al = (a - ah.astype(F32)).astype(BF16)
    wh = w.astype(BF16)
    wl = (w - wh.astype(F32)).astype(BF16)
    return _dot(ah, wh) + (_dot(ah, wl) + _dot(al, wh))


def _ada(c, w_ada, b_ada):
    bsz = c.shape[0]
    n = w_ada.shape[1]
    blk = D_MODEL
    return pl.pallas_call(
        _ada_kernel,
        grid=(n // blk,),
        in_specs=[_const_spec((bsz, D_MODEL)),
                  pl.BlockSpec((D_MODEL, blk), lambda j: (0, j)),
                  pl.BlockSpec((1, blk), lambda j: (0, j))],
        out_specs=pl.BlockSpec((bsz, blk), lambda j: (0, j)),
        out_shape=jax.ShapeDtypeStruct((bsz, n), F32),
        compiler_params=_params("arbitrary"),
        name="ada",
    )(c, w_ada, b_ada.reshape(1, n))


def _rope_kernel(pos_ref, freq_ref, cos_ref, sin_ref):
    ang = pos_ref[...].astype(F32) * freq_ref[...]
    cos_ref[...] = jnp.cos(ang)
    sin_ref[...] = jnp.sin(ang)


def _rope_tables(positions):
    half = QK_ROPE // 2
    t = positions.size
    rows = t * half // 128
    pos_rep = jnp.repeat(positions.reshape(-1), half).reshape(rows, 128)
    inv_freq = ROPE_THETA ** (-jnp.arange(half, dtype=F32) / half)
    freq_row = jnp.tile(inv_freq, 128 // half).reshape(1, 128)
    blk = min(rows, 512)
    cos_d, sin_d = pl.pallas_call(
        _rope_kernel,
        grid=(rows // blk,),
        in_specs=[pl.BlockSpec((blk, 128), lambda i: (i, 0)), _const_spec((1, 128))],
        out_specs=[pl.BlockSpec((blk, 128), lambda i: (i, 0))] * 2,
        out_shape=[jax.ShapeDtypeStruct((rows, 128), F32)] * 2,
        compiler_params=_params("arbitrary"),
        name="rope",
    )(pos_rep, freq_row)
    return jnp.concatenate(
        [cos_d.reshape(t, half), sin_d.reshape(t, half), jnp.zeros((t, 128 - 2 * half), F32)], axis=1)


def _s5par_kernel(lr_ref, li_ref, ldt_ref, bre_ref, bim_ref, ar_ref, ai_ref, bbr_ref, bbi_ref):
    lr = lr_ref[...]
    li = li_ref[...]
    dt = jnp.exp(ldt_ref[...])
    mag = jnp.exp(lr * dt)
    ar = mag * jnp.cos(li * dt)
    ai = mag * jnp.sin(li * dt)
    den = lr * lr + li * li
    nr = ar - 1.0
    ni = ai
    coef_r = (nr * lr + ni * li) / den
    coef_i = (ni * lr - nr * li) / den
    ar_ref[...] = ar
    ai_ref[...] = ai
    bre = bre_ref[...]
    bim = bim_ref[...]
    cr = coef_r[:, None, :]
    ci = coef_i[:, None, :]
    bbr_ref[...] = cr * bre - ci * bim
    bbi_ref[...] = cr * bim + ci * bre


def _s5_params(lam_re, lam_im, log_dt, b_re, b_im):
    g, p = lam_re.shape
    c = b_re.shape[-1]
    bre_t = jnp.transpose(b_re, (0, 2, 1))
    bim_t = jnp.transpose(b_im, (0, 2, 1))
    return pl.pallas_call(
        _s5par_kernel,
        out_shape=[jax.ShapeDtypeStruct((g, p), F32), jax.ShapeDtypeStruct((g, p), F32),
                   jax.ShapeDtypeStruct((g, c, p), F32), jax.ShapeDtypeStruct((g, c, p), F32)],
        name="s5par",
    )(lam_re, lam_im, log_dt.reshape(g, 1), bre_t, bim_t)


def _block_diag(blocks):
    g, r, c = blocks.shape
    eye = jnp.eye(g, dtype=blocks.dtype)
    return (blocks[:, :, None, :] * eye[:, None, :, None]).reshape(g * r, g * c)


@functools.lru_cache(maxsize=None)
def _layout_constants():
    hp, nh = HEAD_PAD, N_HEADS
    width = nh * hp
    half = QK_ROPE // 2
    place = np.zeros((128, width), np.float32)
    hsel = np.zeros((width, 128), np.float32)
    rot = np.zeros((1, width), np.float32)
    ecos = np.zeros((128, width), np.float32)
    esin = np.zeros((128, width), np.float32)
    for h in range(nh):
        base = h * hp
        hsel[base:base + QK_HEAD, h] = 1.0
        for j in range(QK_ROPE):
            place[j, base + QK_NOPE + j] = 1.0
        for j in range(half):
            c1 = base + QK_NOPE + j
            c2 = c1 + half
            rot[0, c1] = 1.0
            ecos[j, c1] = 1.0
            ecos[j, c2] = 1.0
            esin[half + j, c1] = 1.0
            esin[half + j, c2] = 1.0
    ones_nope = (ecos.sum(axis=0, keepdims=True) == 0).astype(np.float32)
    return place, hsel, rot, ecos, esin, ones_nope


def _pad_heads(w, head_dim):
    k = w.shape[0]
    w = w.reshape(k, N_HEADS, head_dim)
    w = jnp.pad(w, ((0, 0), (0, 0), (0, HEAD_PAD - head_dim)))
    return w.reshape(k, N_HEADS * HEAD_PAD)


def _inproj_kernel(x_ref, scale_ref, shift_ref, nmix_ref, win_ref, qan_ref, wuq_ref, kvan_ref,
                   wk_ref, wv_ref, place_ref, hsel_ref, hselt_ref, qg_ref, kg_ref, rot_ref,
                   ecos_ref, esin_ref, nope_ref, cs_ref,
                   u_ref, q_ref, k_ref, v_ref):
    x = x_ref[...]
    h = _rms(x) * nmix_ref[...]
    h = h * (1.0 + scale_ref[0]) + shift_ref[0]
    proj = _dot(h.astype(BF16), win_ref[...])
    u_ref[...] = proj[:, :D_SSM].astype(u_ref.dtype)
    o1 = D_SSM + Q_LORA
    o2 = o1 + KV_LORA
    cq = _rms(proj[:, D_SSM:o1]) * qan_ref[...]
    ckv = (_rms(proj[:, o1:o2]) * kvan_ref[...]).astype(BF16)
    kr = proj[:, o2:]
    q = _dot(cq.astype(BF16), wuq_ref[...])
    k = _dot(ckv, wk_ref[...]) + _dot_split(kr, place_ref[...])
    v_ref[...] = _dot(ckv, wv_ref[...]).astype(v_ref.dtype)

    cs = cs_ref[...]
    cos = _dot_split(cs, ecos_ref[...]) + nope_ref[...]
    sin = _dot_split(cs, esin_ref[...])

    def head_norm_rope(z, gain):
        ssq = _dot_split(z * z, hsel_ref[...])
        r = lax.rsqrt(ssq * (1.0 / QK_HEAD) + EPS)
        zn = z * _dot_split(r, hselt_ref[...]) * gain
        width = zn.shape[1]
        half = QK_ROPE // 2
        rotated = jnp.where(rot_ref[...] > 0.5, -pltpu.roll(zn, width - half, axis=1),
                            pltpu.roll(zn, half, axis=1))
        return zn * cos + rotated * sin

    qr = head_norm_rope(q, qg_ref[...]) * (QK_HEAD ** -0.5)
    q_ref[...] = qr.astype(q_ref.dtype)
    k_ref[...] = head_norm_rope(k, kg_ref[...]).astype(k_ref.dtype)


def _inproj(x2, scale_m, shift_m, norm_mix, w_in, q_a_norm, w_uq, kv_a_norm, w_ukv, q_norm, k_norm,
            cs, bsz, seq, tm):
    t = bsz * seq
    nsb = seq // tm
    place, hsel, rot, ecos, esin, ones_nope = _layout_constants()
    width = N_HEADS * HEAD_PAD
    win = jnp.pad(w_in, ((0, 0), (0, D_MODEL - w_in.shape[1]))).astype(BF16)
    wuq = _pad_heads(w_uq, QK_HEAD).astype(BF16)
    wkv = w_ukv.reshape(KV_LORA, N_HEADS, QK_NOPE + V_HEAD)
    wk = _pad_heads(wkv[:, :, :QK_NOPE].reshape(KV_LORA, N_HEADS * QK_NOPE), QK_NOPE).astype(BF16)
    wv = wkv[:, :, QK_NOPE:].reshape(KV_LORA, N_HEADS * V_HEAD).astype(BF16)
    qg = _pad_heads(jnp.tile(q_norm, N_HEADS).reshape(1, -1), QK_HEAD)
    kg = _pad_heads(jnp.tile(k_norm, N_HEADS).reshape(1, -1), QK_HEAD)
    consts = [
        norm_mix.reshape(1, D_MODEL), win, q_a_norm.reshape(1, Q_LORA), wuq,
        kv_a_norm.reshape(1, KV_LORA), wk, wv,
        jnp.asarray(place, BF16), jnp.asarray(hsel, BF16), jnp.asarray(hsel.T.copy(), BF16), qg, kg,
        jnp.asarray(rot, F32), jnp.asarray(ecos, BF16), jnp.asarray(esin, BF16),
        jnp.asarray(ones_nope, F32),
    ]
    mod_spec = pl.BlockSpec((1, 1, D_MODEL), lambda i: (i // nsb, 0, 0))
    in_specs = ([pl.BlockSpec((tm, D_MODEL), lambda i: (i, 0)), mod_spec, mod_spec]
                + [_const_spec(a.shape) for a in consts]
                + [pl.BlockSpec((tm, 128), lambda i: (i, 0))])
    out_specs = [
        pl.BlockSpec((tm, D_SSM), lambda i: (i % nsb, i // nsb)),
        pl.BlockSpec((tm, width), lambda i: (i, 0)),
        pl.BlockSpec((tm, width), lambda i: (i, 0)),
        pl.BlockSpec((tm, N_HEADS * V_HEAD), lambda i: (i, 0)),
    ]
    out_shape = [
        jax.ShapeDtypeStruct((seq, bsz * D_SSM), BF16),
        jax.ShapeDtypeStruct((t, width), BF16),
        jax.ShapeDtypeStruct((t, width), BF16),
        jax.ShapeDtypeStruct((t, N_HEADS * V_HEAD), BF16),
    ]
    return pl.pallas_call(
        _inproj_kernel, grid=(t // tm,), in_specs=in_specs, out_specs=out_specs, out_shape=out_shape,
        compiler_params=_params("arbitrary"), name="inproj",
    )(x2, scale_m, shift_m, *consts, cs)


def _attn_kernel(q_ref, k_ref, v_ref, o_ref, s_ref, *, tq):
    qi = pl.program_id(2)
    lane = lax.broadcasted_iota(jnp.int32, (1, 2 * V_HEAD), 1)
    first = lane < V_HEAD

    def scores(j, slot):
        kb = k_ref[pl.ds(pl.multiple_of(j * tq, tq), tq), :]
        s_ref[slot, 0] = _dot_nt(q_ref[:, :HEAD_PAD], kb[:, :HEAD_PAD])
        s_ref[slot, 1] = _dot_nt(q_ref[:, HEAD_PAD:], kb[:, HEAD_PAD:])

    def update(j, slot, carry, masked):
        m0, m1, acc0, acc1 = carry
        vb = v_ref[pl.ds(pl.multiple_of(j * tq, tq), tq), :]
        one = jnp.ones_like(vb)
        s0 = s_ref[slot, 0]
        s1 = s_ref[slot, 1]
        if masked:
            row = lax.broadcasted_iota(jnp.int32, (tq, tq), 0)
            col = lax.broadcasted_iota(jnp.int32, (tq, tq), 1)
            keep = col <= row
            s0 = jnp.where(keep, s0, -jnp.inf)
            s1 = jnp.where(keep, s1, -jnp.inf)
        n0 = jnp.maximum(m0, jnp.max(s0, axis=-1, keepdims=True))
        n1 = jnp.maximum(m1, jnp.max(s1, axis=-1, keepdims=True))
        p0 = jnp.exp(s0 - n0).astype(BF16)
        p1 = jnp.exp(s1 - n1).astype(BF16)
        acc0 = jnp.exp(m0 - n0) * acc0 + _dot(p0, jnp.where(first, vb, one))
        acc1 = jnp.exp(m1 - n1) * acc1 + _dot(p1, jnp.where(first, one, vb))
        return n0, n1, acc0, acc1

    def body(j, carry):
        slot = j % 2
        carry = update(j, slot, carry, False)
        scores(j + 1, 1 - slot)
        return carry

    scores(0, 0)
    neg = jnp.full((tq, 1), -jnp.inf, F32)
    zacc = jnp.zeros((tq, 2 * V_HEAD), F32)
    carry = lax.fori_loop(0, qi, body, (neg, neg, zacc, zacc))
    _, _, acc0, acc1 = update(qi, qi % 2, carry, True)
    out0 = acc0 * pltpu.roll(1.0 / acc0, V_HEAD, axis=1)
    out1 = acc1 * pltpu.roll(1.0 / acc1, V_HEAD, axis=1)
    o_ref[...] = jnp.where(first, out0, out1).astype(o_ref.dtype)


def _attention(q, k, v, bsz, seq, tq):
    nq = seq // tq
    t = bsz * seq
    return pl.pallas_call(
        functools.partial(_attn_kernel, tq=tq),
        grid=(bsz, N_HEADS // 2, nq),
        in_specs=[pl.BlockSpec((tq, 2 * HEAD_PAD), lambda b, h, i: (b * nq + i, h)),
                  pl.BlockSpec((seq, 2 * HEAD_PAD), lambda b, h, i: (b, h)),
                  pl.BlockSpec((seq, 2 * V_HEAD), lambda b, h, i: (b, h))],
        out_specs=pl.BlockSpec((tq, 2 * V_HEAD), lambda b, h, i: (b * nq + i, h)),
        out_shape=jax.ShapeDtypeStruct((t, N_HEADS * V_HEAD), BF16),
        scratch_shapes=[pltpu.VMEM((2, 2, tq, tq), F32)],
        compiler_params=_params("arbitrary", "arbitrary", "arbitrary"),
        name="attn",
    )(q, k, v)


def _s5_kernel(u_ref, bd_ref, ar_ref, ai_ref, cd_ref, dsk_ref, wglu_ref, bglu_ref, gain_ref,
               y_ref, bu_ref, st_ref, *, lc, bsz, cb):
    @pl.when(pl.program_id(0) == 0)
    def _():
        st_ref[...] = jnp.zeros_like(st_ref)

    u = u_ref[...]
    bu_ref[...] = _dot(u, bd_ref[...])

    for c0 in range(0, N_STATE, cb):
        ar = ar_ref[:, c0:c0 + cb]
        ai = ai_ref[:, c0:c0 + cb]

        def step(t, carry, c0=c0, ar=ar, ai=ai):
            xr, xi = carry
            r0 = pl.multiple_of(t * bsz, bsz)
            bur = bu_ref[pl.ds(r0, bsz), c0:c0 + cb]
            bui = bu_ref[pl.ds(r0, bsz), N_STATE + c0:N_STATE + c0 + cb]
            nxr = ar * xr - ai * xi + bur
            nxi = ar * xi + ai * xr + bui
            bu_ref[pl.ds(r0, bsz), c0:c0 + cb] = nxr
            bu_ref[pl.ds(r0, bsz), N_STATE + c0:N_STATE + c0 + cb] = nxi
            return nxr, nxi

        xr0 = st_ref[:, c0:c0 + cb]
        xi0 = st_ref[:, N_STATE + c0:N_STATE + c0 + cb]
        xr, xi = lax.fori_loop(0, lc, step, (xr0, xi0))
        st_ref[:, c0:c0 + cb] = xr
        st_ref[:, N_STATE + c0:N_STATE + c0 + cb] = xi

    y = _dot(bu_ref[...].astype(BF16), cd_ref[...]) + dsk_ref[...] * u.astype(F32)
    y = _gelu(y)
    y = y * jax.nn.sigmoid(_dot(y.astype(BF16), wglu_ref[...]) + bglu_ref[...])
    y_ref[...] = (_rms(y) * gain_ref[...]).astype(y_ref.dtype)


def _s5(u_tm, bd, ar, ai, cd, d_skip, w_glu, b_glu, gain, bsz, seq, lc):
    rows = lc * bsz
    consts = [bd, ar, ai, cd, d_skip.reshape(1, D_SSM), w_glu.astype(BF16), b_glu.reshape(1, D_SSM),
              gain.reshape(1, D_SSM)]
    return pl.pallas_call(
        functools.partial(_s5_kernel, lc=lc, bsz=bsz, cb=512),
        grid=(seq // lc,),
        in_specs=[pl.BlockSpec((rows, D_SSM), lambda j: (j, 0))] + [_const_spec(a.shape) for a in consts],
        out_specs=pl.BlockSpec((rows, D_SSM), lambda j: (j, 0)),
        out_shape=jax.ShapeDtypeStruct((seq * bsz, D_SSM), BF16),
        scratch_shapes=[pltpu.VMEM((rows, 2 * N_STATE), F32), pltpu.VMEM((bsz, 2 * N_STATE), F32)],
        compiler_params=_params("arbitrary"),
        name="s5",
    )(u_tm, *consts)


def _outproj_kernel(ys_ref, ya_ref, x_ref, gate_ref, scale_ref, shift_ref, ga_ref, wos_ref, woa_ref,
                    nffn_ref, wq_ref, x1_ref, h2_ref, qp_ref):
    ya = _rms(ya_ref[...].astype(F32)) * ga_ref[...]
    y = _dot(ys_ref[...], wos_ref[...]) + _dot(ya.astype(BF16), woa_ref[...])
    x1 = x_ref[...] + gate_ref[0] * y
    x1_ref[...] = x1
    h2 = _rms(x1) * nffn_ref[...]
    h2 = h2 * (1.0 + scale_ref[0]) + shift_ref[0]
    hi = h2.astype(BF16)
    lo = (h2 - hi.astype(F32)).astype(BF16)
    h2_ref[...] = jnp.concatenate(
        [part[:, c * 128:(c + 1) * 128] for part in (hi, lo) for c in _CHUNK_OF_PACKED], axis=1)
    qp = _dot(hi, wq_ref[...]).astype(qp_ref.dtype)
    for h in range(PEER_HEADS):
        qp_ref[h] = qp[:, h * 2 * PEER_HALF:(h + 1) * 2 * PEER_HALF]


def _outproj(ys_tm, ya, x2, gate_m, scale_f, shift_f, out_norm_attn, w_out, norm_ffn, w_query,
             bsz, seq, tm):
    t = bsz * seq
    nsb = seq // tm
    wo = w_out.astype(BF16)
    consts = [out_norm_attn.reshape(1, -1), wo[:D_SSM], wo[D_SSM:], norm_ffn.reshape(1, D_MODEL),
              w_query.astype(BF16)]
    mod_spec = pl.BlockSpec((1, 1, D_MODEL), lambda i: (i // nsb, 0, 0))
    row_spec = pl.BlockSpec((tm, D_MODEL), lambda i: (i, 0))
    return pl.pallas_call(
        _outproj_kernel,
        grid=(t // tm,),
        in_specs=[pl.BlockSpec((tm, D_SSM), lambda i: (i % nsb, i // nsb)),
                  pl.BlockSpec((tm, D_SSM), lambda i: (i, 0)),
                  row_spec, mod_spec, mod_spec, mod_spec] + [_const_spec(a.shape) for a in consts],
        out_specs=[row_spec, pl.BlockSpec((tm, 2 * D_MODEL), lambda i: (i, 0)),
                   pl.BlockSpec((PEER_HEADS, tm, 2 * PEER_HALF), lambda i: (0, i, 0))],
        out_shape=[jax.ShapeDtypeStruct((t, D_MODEL), F32), jax.ShapeDtypeStruct((t, 2 * D_MODEL), BF16),
                   jax.ShapeDtypeStruct((PEER_HEADS, t, 2 * PEER_HALF), BF16)],
        compiler_params=_params("arbitrary"),
        name="outproj",
    )(ys_tm, ya, x2, gate_m, scale_f, shift_f, *consts)


def _topk_rows(s, k, payload=None):
    n_rows = s.shape[0]
    iota = lax.broadcasted_iota(jnp.int32, s.shape, 0)
    vals, picks = [], []
    for _ in range(k):
        m = jnp.max(s, axis=0, keepdims=True)
        ix = jnp.min(jnp.where(s == m, iota, n_rows), axis=0, keepdims=True)
        hit = iota == ix
        vals.append(m)
        if payload is None:
            picks.append(ix)
        else:
            picks.append(jnp.sum(jnp.where(hit, payload, 0.0), axis=0, keepdims=True))
        s = jnp.where(hit, -jnp.inf, s)
    return jnp.concatenate(vals, axis=0), jnp.concatenate(picks, axis=0)


@functools.lru_cache(maxsize=None)
def _candidate_constants():
    k = PEER_TOPK
    pairs = [(a, b) for a in range(k) for b in range(k) if (a + 1) * (b + 1) <= k]
    rows = -(-len(pairs) // 16) * 16
    sel_a = np.zeros((rows, 128), np.float32)
    sel_b = np.zeros((rows, 128), np.float32)
    pad = np.zeros((rows, 1), np.float32)
    for r, (a, b) in enumerate(pairs):
        sel_a[r, a] = 1.0
        sel_b[r, b] = 1.0
    pad[len(pairs):] = -np.inf
    return sel_a, sel_b, pad


def _select_rows(sel, a):
    hi = a.astype(BF16)
    r1 = a - hi.astype(F32)
    mid = r1.astype(BF16)
    lo = (r1 - mid.astype(F32)).astype(BF16)
    return (_dot(sel, hi) + _dot(sel, mid)) + _dot(sel, lo)


def _topk_kernel(q_ref, keys_ref, sela_ref, selb_ref, pad_ref, idx_ref, g_ref):
    k = PEER_TOPK
    tm = q_ref.shape[1]
    fill = jnp.zeros((128 - k, tm), F32)

    def head(h, _):
        q = q_ref[h]
        s1 = _dot_nt(keys_ref[0], q[:, :PEER_HALF])
        s2 = _dot_nt(keys_ref[1], q[:, PEER_HALF:])
        v1, i1 = _topk_rows(s1, k)
        v2, i2 = _topk_rows(s2, k)
        sela = sela_ref[...]
        selb = selb_ref[...]
        cand = (_select_rows(sela, jnp.concatenate([v1, fill], axis=0))
                + _select_rows(selb, jnp.concatenate([v2, fill], axis=0))) + pad_ref[...]
        e1 = _dot(sela, jnp.concatenate([i1.astype(F32), fill], axis=0).astype(BF16))
        e2 = _dot(selb, jnp.concatenate([i2.astype(F32), fill], axis=0).astype(BF16))
        top_s, top_e = _topk_rows(cand, k, payload=e1 * PEER_KEYS + e2)
        idx_ref[h] = top_e.astype(jnp.int32) * _WORD_ROWS
        e = jnp.exp(top_s - top_s[0:1, :])
        g_ref[h] = e / jnp.sum(e, axis=0, keepdims=True)
        return 0

    lax.fori_loop(0, PEER_HEADS, head, 0)


def _topk(qp, sub_keys, tm):
    t = qp.shape[1]
    sel_a, sel_b, pad = _candidate_constants()
    consts = [sub_keys.astype(BF16), jnp.asarray(sel_a, BF16), jnp.asarray(sel_b, BF16), jnp.asarray(pad, F32)]
    return pl.pallas_call(
        _topk_kernel,
        grid=(t // tm,),
        in_specs=[pl.BlockSpec((PEER_HEADS, tm, 2 * PEER_HALF), lambda i: (0, i, 0))]
        + [_const_spec(a.shape) for a in consts],
        out_specs=[pl.BlockSpec((PEER_HEADS, PEER_TOPK, tm), lambda i: (0, 0, i))] * 2,
        out_shape=[jax.ShapeDtypeStruct((PEER_HEADS, PEER_TOPK, t), jnp.int32),
                   jax.ShapeDtypeStruct((PEER_HEADS, PEER_TOPK, t), F32)],
        compiler_params=_params("arbitrary"),
        name="topk",
    )(qp, *consts)


_CHUNK_OF_PACKED = tuple((q % 2) * 4 + q // 2 for q in range(8))
_WORD_ROWS = D_MODEL // 256


def _pack_table(w):
    e, d = w.shape
    wb = w.astype(BF16)
    pairs = jnp.stack([wb[:, :d // 2], wb[:, d // 2:]], axis=-1)
    return lax.bitcast_convert_type(pairs, jnp.int32).reshape(e * _WORD_ROWS, 128)


@functools.lru_cache(maxsize=None)
def _peer_constants():
    lane_q = np.arange(8 * N_SEL) % 8
    expand = np.zeros((N_SEL, 8 * N_SEL), np.float32)
    expand[np.arange(8 * N_SEL) // 8, np.arange(8 * N_SEL)] = 1.0
    msk_dn = (lane_q[None, :] == np.arange(8)[:, None]).astype(np.float32)
    packed_of_chunk = np.argsort(np.array(_CHUNK_OF_PACKED))
    msk_up = (lane_q[None, :] == packed_of_chunk[:, None]).astype(np.float32)
    return expand, msk_dn, msk_up


_GROUP = 8
_N_WIN_BUF = 4


def _gather_rows(win_ref, buf, q, tbl_ref, g_ref):
    for k in range(N_SEL):
        row = pl.multiple_of(win_ref[buf, q, k], _WORD_ROWS)
        g_ref[_WORD_ROWS * k:_WORD_ROWS * (k + 1), :] = tbl_ref[pl.ds(row, _WORD_ROWS), :]


def _windowed_tokens(tt, idx_vmem, win_ref, sem, stage, compute, finish):
    n_win = tt // _GROUP
    trips = _N_WIN_BUF * _GROUP // 2

    def window_copy(win, buf):
        src = idx_vmem.at[jnp.minimum(win, n_win - 1)]
        return pltpu.make_async_copy(src, win_ref.at[buf], sem.at[buf])

    for b in range(_N_WIN_BUF):
        window_copy(b, b).start()
    window_copy(0, 0).wait()
    stage(0, 0, 0, 0)
    stage(1, 0, 1, 1)

    def body(i, _):
        w0 = _N_WIN_BUF * i
        base = _GROUP * w0
        for p in range(trips):
            t0 = base + 2 * p
            first_staged = 2 * p + 2
            if first_staged % _GROUP == 0:
                nw = first_staged // _GROUP
                window_copy(w0 + nw, nw % _N_WIN_BUF).wait()
                window_copy(w0 + nw - 1 + _N_WIN_BUF, (nw - 1) % _N_WIN_BUF).start()
            r0 = compute(t0, 0)
            r1 = compute(t0 + 1, 1)
            for slot in range(2):
                u = first_staged + slot
                stage(jnp.minimum(base + u, tt - 1), (u // _GROUP) % _N_WIN_BUF, u % _GROUP, slot)
            finish(t0, r0)
            finish(t0 + 1, r1)
        return 0

    lax.fori_loop(0, n_win // _N_WIN_BUF, body, 0)
    for b in range(1, _N_WIN_BUF):
        window_copy(n_win + b, b).wait()


def _peer_dn_kernel(idx_vmem, hm_ref, tbl_ref, msk_ref, et_ref, a_ref, win_ref, sem, y_ref, *g_refs, tt):
    def stage(t, buf, q, slot):
        _gather_rows(win_ref, buf, q, tbl_ref, g_refs[slot])

    def compute(t, slot):
        rows = pltpu.bitcast(g_refs[slot][...], BF16)
        return _dot_nt(hm_ref[t], rows)

    def finish(t, y):
        y_ref[t] = y[:8] + y[8:]

    _windowed_tokens(tt, idx_vmem, win_ref, sem, stage, compute, finish)
    s = jnp.sum(y_ref[...] * msk_ref[...], axis=1)
    a_ref[...] = _dot_split(s, et_ref[...])


def _peer_dn(idx, hm, tbl, tt):
    t = hm.shape[0]
    expand, msk_dn, _ = _peer_constants()
    consts = [jnp.asarray(msk_dn, F32), jnp.asarray(expand.T.copy(), BF16)]
    return pl.pallas_call(
        functools.partial(_peer_dn_kernel, tt=tt),
        grid=(t // tt,),
        in_specs=[pl.BlockSpec((tt // _GROUP, _GROUP, N_SEL), lambda i: (i, 0, 0)),
                  pl.BlockSpec((tt, 16, 128), lambda i: (i, 0, 0)),
                  pl.BlockSpec(memory_space=pltpu.VMEM)] + [_const_spec(a.shape) for a in consts],
        out_specs=pl.BlockSpec((tt, N_SEL), lambda i: (i, 0)),
        out_shape=jax.ShapeDtypeStruct((t, N_SEL), F32),
        scratch_shapes=[pltpu.SMEM((_N_WIN_BUF, _GROUP, N_SEL), jnp.int32), pltpu.SemaphoreType.DMA((_N_WIN_BUF,)),
                        pltpu.VMEM((tt, 8, 8 * N_SEL), F32)] + _gather_buffers(),
        compiler_params=_params("arbitrary"),
        name="peer_dn",
    )(idx, hm, tbl, *consts)


def _gate_kernel(a_ref, g_ref, w_ref):
    w_ref[...] = g_ref[...] * _gelu(a_ref[...])


def _gate(a, g):
    t = a.shape[0]
    blk = min(t, 2048)
    spec = pl.BlockSpec((blk, N_SEL), lambda i: (i, 0))
    return pl.pallas_call(
        _gate_kernel, grid=(t // blk,), in_specs=[spec, spec], out_specs=spec,
        out_shape=jax.ShapeDtypeStruct((t, N_SEL), F32),
        compiler_params=_params("arbitrary"), name="gate",
    )(a, g)


def _gather_buffers():
    return [pltpu.VMEM((_WORD_ROWS * N_SEL, 128), jnp.int32) for _ in range(2)]


def _peer_up_kernel(idx_vmem, w_ref, x1_ref, gate_ref, tbl_ref, e_ref, msk_ref, o_ref,
                    win_ref, sem, wexp_ref, lhs_ref, *g_refs, tt):
    wexp_ref[...] = _dot_split(w_ref[...], e_ref[...])
    gate = gate_ref[0]
    msk = msk_ref[...]
    top = lax.broadcasted_iota(jnp.int32, msk.shape, 0) < 8

    def stage(t, buf, q, slot):
        _gather_rows(win_ref, buf, q, tbl_ref, g_refs[slot])
        wsel = wexp_ref[pl.ds(t, 1), :] * msk
        hi = wsel.astype(BF16)
        lo = (wsel - hi.astype(F32)).astype(BF16)
        lhs_ref[slot] = jnp.where(top, hi, lo)

    def compute(t, slot):
        rows = pltpu.bitcast(g_refs[slot][...], BF16)
        return _dot(lhs_ref[slot], rows)

    def finish(t, o16):
        o_ref[t] = x1_ref[t] + gate * (o16[:8] + o16[8:])

    _windowed_tokens(tt, idx_vmem, win_ref, sem, stage, compute, finish)


def _peer_up(idx, wgt, x13, gate3, tbl, seq, tt):
    t = x13.shape[0]
    nsb = seq // tt
    expand, _, msk_up = _peer_constants()
    consts = [jnp.asarray(expand, BF16), jnp.asarray(np.concatenate([msk_up, msk_up], axis=0), F32)]
    row = pl.BlockSpec((tt, 8, 128), lambda i: (i, 0, 0))
    return pl.pallas_call(
        functools.partial(_peer_up_kernel, tt=tt),
        grid=(t // tt,),
        in_specs=[pl.BlockSpec((tt // _GROUP, _GROUP, N_SEL), lambda i: (i, 0, 0)),
                  pl.BlockSpec((tt, N_SEL), lambda i: (i, 0)),
                  row, pl.BlockSpec((1, 8, 128), lambda i: (i // nsb, 0, 0)),
                  pl.BlockSpec(memory_space=pltpu.VMEM)] + [_const_spec(a.shape) for a in consts],
        out_specs=row,
        out_shape=jax.ShapeDtypeStruct((t, 8, 128), F32),
        scratch_shapes=[pltpu.SMEM((_N_WIN_BUF, _GROUP, N_SEL), jnp.int32), pltpu.SemaphoreType.DMA((_N_WIN_BUF,)),
                        pltpu.VMEM((tt, 8 * N_SEL), F32), pltpu.VMEM((2, 16, 8 * N_SEL), BF16)]
        + _gather_buffers(),
        compiler_params=_params("arbitrary"),
        name="peer_up",
    )(idx, wgt, x13, gate3, tbl, *consts)


def _layer(x, c, positions, w_ada, b_ada, norm_mix, norm_ffn, w_in, lam_re, lam_im, log_dt, b_re, b_im,
           c_re, c_im, d_skip, w_glu, b_glu, q_a_norm, w_uq, kv_a_norm, w_ukv, q_norm, k_norm,
           out_norm_ssm, out_norm_attn, w_out, w_query, sub_keys, expert_down, expert_up):
    bsz, seq, _ = x.shape
    t = bsz * seq
    tm = min(256, seq)
    x2 = x.reshape(t, D_MODEL)

    mod = _ada(c, w_ada, b_ada)
    shift_m, scale_m, gate_m, shift_f, scale_f, gate_f = [
        m.reshape(bsz, 1, D_MODEL) for m in jnp.split(mod, N_ADA, axis=-1)]

    cs = _rope_tables(positions)
    u_tm, q, k, v = _inproj(x2, scale_m, shift_m, norm_mix, w_in, q_a_norm, w_uq, kv_a_norm, w_ukv,
                            q_norm, k_norm, cs, bsz, seq, tm)
    ya = _attention(q, k, v, bsz, seq, min(512, seq))

    ar, ai, bbr, bbi = _s5_params(lam_re, lam_im, log_dt, b_re, b_im)
    bd = jnp.concatenate([_block_diag(bbr), _block_diag(bbi)], axis=1).astype(BF16)
    cd = jnp.concatenate([_block_diag(jnp.transpose(c_re, (0, 2, 1))),
                          _block_diag(jnp.transpose(-c_im, (0, 2, 1)))], axis=0).astype(BF16)
    ys_tm = _s5(u_tm.reshape(seq * bsz, D_SSM), bd, ar.reshape(1, N_STATE), ai.reshape(1, N_STATE), cd,
                d_skip, w_glu, b_glu, out_norm_ssm, bsz, seq, min(32, seq))

    x1, hm, qp = _outproj(ys_tm.reshape(seq, bsz * D_SSM), ya, x2, gate_m, scale_f, shift_f,
                          out_norm_attn, w_out, norm_ffn, w_query, bsz, seq, tm)

    idx_t, g_t = _topk(qp, sub_keys, tm)
    idx = jnp.transpose(idx_t.reshape(N_SEL, t))
    g = jnp.transpose(g_t.reshape(N_SEL, t))
    idx_win = idx.reshape(t // _GROUP, _GROUP, N_SEL)

    tt = min(128, seq)
    a = _peer_dn(idx_win, hm.reshape(t, 16, 128), _pack_table(expert_down), tt)
    wgt = _gate(a, g)
    out = _peer_up(idx_win, wgt, x1.reshape(t, 8, 128), gate_f.reshape(bsz, 8, 128),
                   _pack_table(expert_up), seq, tt)
    return out.reshape(bsz, seq, D_MODEL)


def kernel(x, c, positions, w_ada, b_ada, norm_mix, norm_ffn, w_in, lam_re, lam_im, log_dt, b_re, b_im, c_re, c_im, d_skip, w_glu, b_glu, q_a_norm, w_uq, kv_a_norm, w_ukv, q_norm, k_norm, out_norm_ssm, out_norm_attn, w_out, w_query, sub_keys, expert_down, expert_up):
    for l in range(w_ada.shape[0]):
        x = _layer(x, c, positions, w_ada[l], b_ada[l], norm_mix[l], norm_ffn[l], w_in[l],
                   lam_re[l], lam_im[l], log_dt[l], b_re[l], b_im[l], c_re[l], c_im[l],
                   d_skip[l], w_glu[l], b_glu[l], q_a_norm[l], w_uq[l], kv_a_norm[l], w_ukv[l],
                   q_norm[l], k_norm[l], out_norm_ssm[l], out_norm_attn[l], w_out[l],
                   w_query[l], sub_keys[l], expert_down[l], expert_up[l])
    return x
```
